```python
import math
import jax, jax.numpy as jnp
from jax import lax
import numpy as np

D_MODEL = 1024
BATCH = 8
SEQ = 4096
DEPTH = 4

P_DIM = 256
D_FF = 2816
ML_HEADS = 4
ML_QK_DIM = 64
ML_V_DIM = 128
ML_CHUNK = 64
SW_Q_HEADS = 8
SW_KV_HEADS = 2
SW_HEAD_DIM = 64
WINDOW = 128
REL_BUCKETS = 32
REL_MAX_DIST = 128
EPS = 1e-6

ML_QK_W = ML_HEADS * ML_QK_DIM
ML_V_W = ML_HEADS * ML_V_DIM
SW_Q_W = SW_Q_HEADS * SW_HEAD_DIM
SW_KV_W = SW_KV_HEADS * SW_HEAD_DIM
SPLITS = (ML_QK_W, ML_QK_W, ML_V_W, ML_V_W, ML_HEADS, ML_HEADS, SW_Q_W, SW_KV_W, SW_KV_W, D_MODEL, D_MODEL)
D_IN = 2 * ML_QK_W + 2 * ML_V_W + 2 * ML_HEADS + SW_Q_W + 2 * SW_KV_W + 2 * D_MODEL

kernel_name = "hybrid_mlstm_swa_sink_macaron"


def rmsnorm(x, g):
    xf = x.astype(jnp.float32)
    y = xf * lax.rsqrt(jnp.mean(xf * xf, axis=-1, keepdims=True) + EPS)
    return (y * g.astype(jnp.float32)).astype(x.dtype)


def swiglu_ffn(x, wi, wo):
    g, u = jnp.split(x @ wi, 2, axis=-1)
    return (jax.nn.silu(g) * u) @ wo


def t5_bucket(dist):
    max_exact = REL_BUCKETS // 2
    d = np.maximum(dist, 0)
    large = max_exact + (np.log(np.maximum(d, 1) / max_exact) / np.log(REL_MAX_DIST / max_exact)
                         * (REL_BUCKETS - max_exact)).astype(np.int32)
    large = np.minimum(large, REL_BUCKETS - 1)
    return np.where(d < max_exact, d, large).astype(np.int32)


def mlstm_chunkwise(q, k, v, i_pre, f_pre):
    B, S, H, dk = q.shape
    dv = v.shape[-1]
    L = ML_CHUNK
    NC = S // L
    f32 = jnp.float32
    qc = q.astype(f32).reshape(B, NC, L, H, dk)
    kc = (k.astype(f32) / math.sqrt(dk)).reshape(B, NC, L, H, dk)
    vc = v.astype(f32).reshape(B, NC, L, H, dv)
    log_i = jnp.swapaxes(i_pre.astype(f32).reshape(B, NC, L, H), 2, 3)
    log_f = jnp.swapaxes(jax.nn.log_sigmoid(f_pre.astype(f32)).reshape(B, NC, L, H), 2, 3)
    b = jnp.cumsum(log_f, axis=-1)
    g = b[..., -1]
    a = g[..., None] - b + log_i
    m_loc = jnp.max(a, axis=-1)
    w = jnp.exp(a - m_loc[..., None])
    C_loc = jnp.einsum('bnhl,bnlhk,bnlhv->bnhkv', w, kc, vc)
    n_loc = jnp.einsum('bnhl,bnlhk->bnhk', w, kc)

    def step(carry, xs):
        C, n, m = carry
        g_c, m_c, C_c, n_c = xs
        m_new = jnp.maximum(g_c + m, m_c)
        s_old = jnp.exp(g_c + m - m_new)
        s_new = jnp.exp(m_c - m_new)
        C_new = s_old[..., None, None] * C + s_new[..., None, None] * C_c
        n_new = s_old[..., None] * n + s_new[..., None] * n_c
        return (C_new, n_new, m_new), (C, n, m)

    init = (jnp.zeros((B, H, dk, dv), f32), jnp.zeros((B, H, dk), f32), jnp.zeros((B, H), f32))
    xs = (jnp.moveaxis(g, 1, 0), jnp.moveaxis(m_loc, 1, 0), jnp.moveaxis(C_loc, 1, 0), jnp.moveaxis(n_loc, 1, 0))
    _, (C_prev, n_prev, m_prev) = lax.scan(step, init, xs)
    C_prev = jnp.moveaxis(C_prev, 0, 1)
    n_prev = jnp.moveaxis(n_prev, 0, 1)
    m_prev = jnp.moveaxis(m_prev, 0, 1)

    causal = jnp.tril(jnp.ones((L, L), dtype=bool))
    D = b[..., :, None] - b[..., None, :] + log_i[..., None, :]
    D = jnp.where(causal, D, -jnp.inf)
    e = b + m_prev[..., None]
    m_t = jnp.maximum(jnp.max(D, axis=-1), e)
    W = jnp.exp(D - m_t[..., None]) * jnp.einsum('bnthd,bnshd->bnhts', qc, kc)
    s_inter = jnp.exp(e - m_t)
    num = (jnp.einsum('bnhts,bnshv->bnthv', W, vc)
           + jnp.swapaxes(s_inter, 2, 3)[..., None] * jnp.einsum('bnthd,bnhdv->bnthv', qc, C_prev))
    den = jnp.sum(W, axis=-1) + s_inter * jnp.einsum('bnthd,bnhd->bnht', qc, n_prev)
    denom = jnp.maximum(jnp.abs(den), jnp.exp(-m_t))
    h = num / jnp.swapaxes(denom, 2, 3)[..., None]
    return h.reshape(B, S, H, dv).astype(q.dtype)


def swa_sink_attention(q, k, v, q_gain, k_gain, sinks, rel_bias):
    B, S, Hq, d = q.shape
    Hkv = k.shape[2]
    G = Hq // Hkv
    Wn = WINDOW
    NB = S // Wn
    f32 = jnp.float32
    q = rmsnorm(q, q_gain)
    k = rmsnorm(k, k_gain)
    qb = q.astype(f32).reshape(B, NB, Wn, Hkv, G, d) * (d ** -0.5)
    kb = k.astype(f32).reshape(B, NB, Wn, Hkv, d)
    vb = v.astype(f32).reshape(B, NB, Wn, Hkv, d)
    shift = lambda t: jnp.concatenate([jnp.zeros_like(t[:, :1]), t[:, :-1]], axis=1)
    kk = jnp.concatenate([shift(kb), kb], axis=2)
    vv = jnp.concatenate([shift(vb), vb], axis=2)
    logits = jnp.einsum('bnqkgd,bnskd->bnkgqs', qb, kk)
    dist = np.arange(Wn)[:, None] + Wn - np.arange(2 * Wn)[None, :]
    in_window = (dist >= 0) & (dist < Wn)
    bias = rel_bias.astype(f32)[t5_bucket(dist)]
    bias = jnp.transpose(bias, (2, 0, 1)).reshape(Hkv, G, Wn, 2 * Wn)
    key_pos = np.arange(NB)[:, None] * Wn - Wn + np.arange(2 * Wn)[None, :]
    valid = in_window[None] & (key_pos >= 0)[:, None, :]
    logits = jnp.where(valid[None, :, None, None], logits + bias, -jnp.inf)
    sink = sinks.astype(f32).reshape(Hkv, G)[None, None, :, :, None, None]
    m = jnp.maximum(jnp.max(logits, axis=-1, keepdims=True), sink)
    pr = jnp.exp(logits - m)
    denom = jnp.sum(pr, axis=-1, keepdims=True) + jnp.exp(sink - m)
    out = jnp.einsum('bnkgqs,bnskd->bnqkgd', pr / denom, vv)
    return out.reshape(B, S, Hq * d).astype(q.dtype)


def setup_inputs(seed: int = 0) -> dict:
    key = jax.random.key(seed)
    ks = jax.random.split(key, 24)
    f32 = jnp.float32
    nrm = lambda k, shape, fan_in: jax.random.normal(k, shape, f32) * (fan_in ** -0.5)
    gain = lambda k, shape: 1.0 + 0.05 * jax.random.normal(k, shape, f32)
    return {
        "x": jax.random.normal(ks[0], (BATCH, SEQ, D_MODEL), f32),
        "p": jax.random.normal(ks[1], (DEPTH, BATCH, SEQ, P_DIM), f32),
        "ffn1_norm": gain(ks[2], (DEPTH, D_MODEL)),
        "ffn1_wi": nrm(ks[3], (DEPTH, D_MODEL, 2 * D_FF), D_MODEL),
        "ffn1_wo": nrm(ks[4], (DEPTH, D_FF, D_MODEL), D_FF),
        "mix_norm": gain(ks[5], (DEPTH, D_MODEL)),
        "w_in": nrm(ks[6], (DEPTH, D_MODEL, D_IN), D_MODEL),
        "b_igate": 0.1 * jax.random.normal(ks[7], (DEPTH, ML_HEADS), f32),
        "b_fgate": jnp.linspace(3.0, 6.0, ML_HEADS, dtype=f32)[None, :] + 0.1 * jax.random.normal(ks[8], (DEPTH, ML_HEADS), f32),
        "ml_out_norm": gain(ks[9], (DEPTH, ML_HEADS, ML_V_DIM)),
        "q_norm": gain(ks[10], (DEPTH, SW_HEAD_DIM)),
        "k_norm": gain(ks[11], (DEPTH, SW_HEAD_DIM)),
        "sinks": 0.5 * jax.random.normal(ks[12], (DEPTH, SW_Q_HEADS), f32),
        "rel_bias": 0.5 * jax.random.normal(ks[13], (REL_BUCKETS, SW_Q_HEADS), f32),
        "w_a": nrm(ks[14], (DEPTH, ML_V_W, D_MODEL), ML_V_W),
        "w_b": nrm(ks[15], (DEPTH, SW_Q_W, D_MODEL), SW_Q_W),
        "w_out": nrm(ks[16], (DEPTH, D_MODEL, D_MODEL), D_MODEL),
        "ffn2_norm": gain(ks[17], (DEPTH, D_MODEL)),
        "ffn2_wi": nrm(ks[18], (DEPTH, D_MODEL, 2 * D_FF), D_MODEL),
        "ffn2_wo": nrm(ks[19], (DEPTH, D_FF, D_MODEL), D_FF),
        "ple_norm": gain(ks[20], (DEPTH, D_MODEL)),
        "w_ple_gate": nrm(ks[21], (DEPTH, D_MODEL, D_MODEL), D_MODEL),
        "w_ple": nrm(ks[22], (DEPTH, P_DIM, D_MODEL), P_DIM),
    }


def reference(x, p, ffn1_norm, ffn1_wi, ffn1_wo, mix_norm, w_in, b_igate, b_fgate, ml_out_norm,
              q_norm, k_norm, sinks, rel_bias, w_a, w_b, w_out, ffn2_norm, ffn2_wi, ffn2_wo,
              ple_norm, w_ple_gate, w_ple):
    B, S, _ = x.shape
    offsets = [int(o) for o in np.cumsum(SPLITS)[:-1]]
    for i in range(DEPTH):
        x = x + 0.5 * swiglu_ffn(rmsnorm(x, ffn1_norm[i]), ffn1_wi[i], ffn1_wo[i])
        u = rmsnorm(x, mix_norm[i])
        z = u @ w_in[i]
        mq, mk, mv, mo, mi, mf, sq, sk, sv, ga, gb = jnp.split(z, offsets, axis=-1)
        hA = mlstm_chunkwise(mq.reshape(B, S, ML_HEADS, ML_QK_DIM),
                             mk.reshape(B, S, ML_HEADS, ML_QK_DIM),
                             mv.reshape(B, S, ML_HEADS, ML_V_DIM),
                             mi + b_igate[i], mf + b_fgate[i])
        hA = rmsnorm(hA, ml_out_norm[i]).reshape(B, S, ML_V_W)
        yA = (jax.nn.sigmoid(mo) * hA) @ w_a[i]
        hB = swa_sink_attention(sq.reshape(B, S, SW_Q_HEADS, SW_HEAD_DIM),
                                sk.reshape(B, S, SW_KV_HEADS, SW_HEAD_DIM),
                                sv.reshape(B, S, SW_KV_HEADS, SW_HEAD_DIM),
                                q_norm[i], k_norm[i], sinks[i], rel_bias)
        yB = hB @ w_b[i]
        mixed = jax.nn.sigmoid(ga) * yA + jax.nn.sigmoid(gb) * yB
        x = x + mixed @ w_out[i]
        x = x + 0.5 * swiglu_ffn(rmsnorm(x, ffn2_norm[i]), ffn2_wi[i], ffn2_wo[i])
        gate = jax.nn.sigmoid(rmsnorm(x, ple_norm[i]) @ w_ple_gate[i])
        x = x + gate * (p[i] @ w_ple[i])
    return x
```

```python
import functools
import math

import jax
import jax.numpy as jnp
import numpy as np
from jax.experimental import pallas as pl
from jax.experimental.pallas import tpu as pltpu

D_MODEL = 1024
P_DIM = 256
D_FF = 2816
ML_HEADS = 4
ML_QK_DIM = 64
ML_V_DIM = 128
SW_Q_HEADS = 8
SW_KV_HEADS = 2
SW_HEAD_DIM = 64
WINDOW = 128
REL_BUCKETS = 32
REL_MAX_DIST = 128
EPS = 1e-6

ML_QK_W = ML_HEADS * ML_QK_DIM
ML_V_W = ML_HEADS * ML_V_DIM
SW_Q_W = SW_Q_HEADS * SW_HEAD_DIM
SW_KV_W = SW_KV_HEADS * SW_HEAD_DIM
SW_GROUP = SW_Q_HEADS // SW_KV_HEADS
SPLITS = (ML_QK_W, ML_QK_W, ML_V_W, ML_V_W, ML_HEADS, ML_HEADS, SW_Q_W, SW_KV_W, SW_KV_W, D_MODEL, D_MODEL)

V7X_LANES = 128
V7X_VMEM_LIMIT_BYTES = 56 * 1024 * 1024

FFN_CHUNK = 256
ML_CHUNK = 128
TOKEN_TILE = 512

F32 = jnp.float32
BF16 = jnp.bfloat16


def _dot(a, b):
    return jnp.dot(a, b, preferred_element_type=F32)


def lax_rsqrt(v):
    return jax.lax.rsqrt(v)


def _rms_scale(xf):
    return lax_rsqrt(jnp.mean(xf * xf, axis=-1, keepdims=True) + EPS)


def _sigmoid(v):
    return 1.0 / (1.0 + jnp.exp(-v))


def _const_spec(shape):
    nd = len(shape)
    return pl.BlockSpec(shape, lambda *_: (0,) * nd, pipeline_mode=pl.Buffered(1))


def _params(*sem):
    return pltpu.CompilerParams(dimension_semantics=sem, vmem_limit_bytes=V7X_VMEM_LIMIT_BYTES)


def _ffn_body(x_ref, g_ref, wi_ref, wo_ref, o_ref, n_ref, act_ref):
    xf = x_ref[...]
    n_ref[...] = (xf * _rms_scale(xf) * g_ref[...]).astype(BF16)
    for j in range(D_FF // FFN_CHUNK):
        lo = j * FFN_CHUNK
        hg = _dot(n_ref[...], wi_ref[:, lo:lo + FFN_CHUNK])
        hu = _dot(n_ref[...], wi_ref[:, D_FF + lo:D_FF + lo + FFN_CHUNK])
        act_ref[:, lo:lo + FFN_CHUNK] = (hg * _sigmoid(hg) * hu).astype(BF16)
    o_ref[...] = x_ref[...] + 0.5 * _dot(act_ref[...], wo_ref[...])


def _ffn(x2, g, wi, wo, tm):
    n_tok = x2.shape[0]
    return pl.pallas_call(
        _ffn_body,
        grid=(n_tok // tm,),
        in_specs=[
            pl.BlockSpec((tm, D_MODEL), lambda i: (i, 0)),
            _const_spec((1, D_MODEL)),
            _const_spec((D_MODEL, 2 * D_FF)),
            _const_spec((D_FF, D_MODEL)),
        ],
        out_specs=pl.BlockSpec((tm, D_MODEL), lambda i: (i, 0)),
        out_shape=jax.ShapeDtypeStruct((n_tok, D_MODEL), F32),
        scratch_shapes=[pltpu.VMEM((tm, D_MODEL), BF16), pltpu.VMEM((tm, D_FF), BF16)],
        compiler_params=_params("parallel"),
        name="ffn",
    )(x2, g, wi, wo)


def _inproj_body(x_ref, g_ref, wm_ref, wkt_ref, wift_ref, ws_ref, wg_ref, gain_ref, ones_ref,
                 mq_ref, mkt_ref, mv_ref, mo_ref, gt_ref, sq_ref, skt_ref, sv_ref, ga_ref, gb_ref):
    xf = x_ref[...]
    u = (xf * _rms_scale(xf) * g_ref[...]).astype(BF16)
    zm = _dot(u, wm_ref[...])
    mq_ref[...] = zm[:, :ML_QK_W].astype(BF16)
    mv_ref[...] = zm[:, ML_QK_W:ML_QK_W + ML_V_W].astype(BF16)
    mo_ref[...] = zm[:, ML_QK_W + ML_V_W:].astype(BF16)
    nt = (((1,), (1,)), ((), ()))
    mkt_ref[...] = (jax.lax.dot_general(wkt_ref[...], u, nt, preferred_element_type=F32)
                    * (1.0 / math.sqrt(ML_QK_DIM))).astype(BF16)
    gt_ref[...] = jax.lax.dot_general(wift_ref[...], u, nt, preferred_element_type=F32)
    zs = _dot(u, ws_ref[...])
    qk = zs[:, :SW_Q_W + SW_KV_W]
    ssq = _dot((qk * qk).astype(BF16), ones_ref[...])
    qkn = qk * lax_rsqrt(ssq * (1.0 / SW_HEAD_DIM) + EPS) * gain_ref[...]
    sq_ref[...] = qkn[:, :SW_Q_W].astype(BF16)
    skt_ref[...] = qkn[:, SW_Q_W:].T.astype(BF16)
    sv_ref[...] = zs[:, SW_Q_W + SW_KV_W:].astype(BF16)
    zg = _dot(u, wg_ref[...])
    ga_ref[...] = zg[:, :D_MODEL].astype(BF16)
    gb_ref[...] = zg[:, D_MODEL:].astype(BF16)


def _inproj(x3, g, wm, wkt, wift, ws, wg, gain, ones, tm):
    b, s, _ = x3.shape
    tok = lambda w: pl.BlockSpec((None, tm, w), lambda i, j: (i, j, 0))
    tok_t = lambda w: pl.BlockSpec((None, w, tm), lambda i, j: (i, 0, j))
    sd = jax.ShapeDtypeStruct
    return pl.pallas_call(
        _inproj_body,
        grid=(b, s // tm),
        in_specs=[tok(D_MODEL)] + [_const_spec(a.shape) for a in (g, wm, wkt, wift, ws, wg, gain, ones)],
        out_specs=[tok(ML_QK_W), tok_t(ML_QK_W), tok(ML_V_W), tok(ML_V_W), tok_t(2 * ML_HEADS),
                   tok(SW_Q_W), tok_t(SW_KV_W), tok(SW_KV_W), tok(D_MODEL), tok(D_MODEL)],
        out_shape=[sd((b, s, ML_QK_W), BF16), sd((b, ML_QK_W, s), BF16), sd((b, s, ML_V_W), BF16),
                   sd((b, s, ML_V_W), BF16), sd((b, 2 * ML_HEADS, s), F32),
                   sd((b, s, SW_Q_W), BF16), sd((b, SW_KV_W, s), BF16), sd((b, s, SW_KV_W), BF16),
                   sd((b, s, D_MODEL), BF16), sd((b, s, D_MODEL), BF16)],
        compiler_params=_params("parallel", "parallel"),
        name="inproj",
    )(x3, g, wm, wkt, wift, ws, wg, gain, ones)


def _mlstm_body(q_ref, kt_ref, v_ref, o_ref, gt_ref, bias_ref, gain_ref, h_ref, c_ref, m_ref):
    L = ML_CHUNK

    @pl.when(pl.program_id(1) == 0)
    def _():
        c_ref[...] = jnp.zeros_like(c_ref)
        m_ref[...] = jnp.zeros_like(m_ref)

    pre = gt_ref[...] + bias_ref[...]
    logf8 = jnp.minimum(pre, 0.0) - jnp.log1p(jnp.exp(-jnp.abs(pre)))
    lane = jax.lax.broadcasted_iota(jnp.int32, (2 * ML_HEADS, L), 1)
    b8 = logf8
    k = 1
    while k < L:
        b8 = b8 + jnp.where(lane >= k, pltpu.roll(b8, k, 1), 0.0)
        k *= 2

    t_idx = jax.lax.broadcasted_iota(jnp.int32, (L, L), 0)
    s_idx = jax.lax.broadcasted_iota(jnp.int32, (L, L), 1)
    causal = s_idx <= t_idx
    ones_col = (jax.lax.broadcasted_iota(jnp.int32, (L, V7X_LANES), 1) == 0).astype(BF16)

    for h in range(ML_HEADS):
        m_prev = m_ref[h:h + 1, 0:1]
        q = q_ref[:, h * ML_QK_DIM:(h + 1) * ML_QK_DIM]
        kt = kt_ref[h * ML_QK_DIM:(h + 1) * ML_QK_DIM, :]
        v_aug = jnp.concatenate([v_ref[:, h * ML_V_DIM:(h + 1) * ML_V_DIM], ones_col], axis=1)
        c_aug = c_ref[h]
        logf = logf8[ML_HEADS + h:ML_HEADS + h + 1, :]
        b_row = b8[ML_HEADS + h:ML_HEADS + h + 1, :]
        g_tot = b_row[:, L - 1:L]
        c_row = pre[h:h + 1, :] - b_row
        a_row = g_tot + c_row
        m_loc = jnp.max(a_row, axis=-1, keepdims=True)

        dm = jnp.where(causal, c_row, -jnp.inf)
        m_col = jnp.maximum(jnp.max(dm, axis=-1, keepdims=True), m_prev)
        b_col = jnp.sum(jnp.where(causal, logf, 0.0), axis=-1, keepdims=True)
        w = jnp.exp(dm - m_col) * _dot(q, kt)
        s_inter = jnp.exp(m_prev - m_col)
        res = _dot(w.astype(BF16), v_aug) + s_inter * _dot(q, c_aug.astype(BF16))
        den = res[:, ML_V_DIM:ML_V_DIM + 1]
        denom = jnp.maximum(jnp.abs(den), jnp.exp(-(b_col + m_col)))
        hh = res[:, :ML_V_DIM] / denom
        hn = hh * lax_rsqrt(jnp.mean(hh * hh, axis=-1, keepdims=True) + EPS) * gain_ref[h:h + 1, :]
        gate = _sigmoid(o_ref[:, h * ML_V_DIM:(h + 1) * ML_V_DIM].astype(F32))
        h_ref[:, h * ML_V_DIM:(h + 1) * ML_V_DIM] = (gate * hn).astype(BF16)

        m_new = jnp.maximum(g_tot + m_prev, m_loc)
        kw = (kt.astype(F32) * jnp.exp(a_row - m_new)).astype(BF16)
        c_ref[h] = jnp.exp(g_tot + m_prev - m_new) * c_aug + _dot(kw, v_aug)
        m_ref[h:h + 1, :] = jnp.broadcast_to(m_new, (1, V7X_LANES))


def _mlstm(mq, mkt, mv, mo, gt, bias, gain):
    b, s, _ = mq.shape
    L = ML_CHUNK
    return pl.pallas_call(
        _mlstm_body,
        grid=(b, s // L),
        in_specs=[
            pl.BlockSpec((None, L, ML_QK_W), lambda i, j: (i, j, 0)),
            pl.BlockSpec((None, ML_QK_W, L), lambda i, j: (i, 0, j)),
            pl.BlockSpec((None, L, ML_V_W), lambda i, j: (i, j, 0)),
            pl.BlockSpec((None, L, ML_V_W), lambda i, j: (i, j, 0)),
            pl.BlockSpec((None, 2 * ML_HEADS, L), lambda i, j: (i, 0, j)),
            _const_spec((2 * ML_HEADS, 1)),
            _const_spec((ML_HEADS, ML_V_DIM)),
        ],
        out_specs=pl.BlockSpec((None, L, ML_V_W), lambda i, j: (i, j, 0)),
        out_shape=jax.ShapeDtypeStruct((b, s, ML_V_W), BF16),
        scratch_shapes=[pltpu.VMEM((ML_HEADS, ML_QK_DIM, 2 * ML_V_DIM), F32),
                        pltpu.VMEM((2 * ML_HEADS, V7X_LANES), F32)],
        compiler_params=_params("parallel", "arbitrary"),
        name="mlstm",
    )(mq, mkt, mv, mo, gt, bias, gain)


def _swa_body(q_ref, ktp_ref, ktc_ref, vp_ref, vc_ref, bias_ref, sink_ref, o_ref):
    W = WINDOW
    d = SW_HEAD_DIM
    lane = jax.lax.broadcasted_iota(jnp.int32, (W, 2 * d), 1)
    zeros_kt = jnp.zeros((d, 2 * W), BF16)
    for g in range(SW_KV_HEADS):
        kt = jnp.concatenate([ktp_ref[g * d:(g + 1) * d, :], ktc_ref[g * d:(g + 1) * d, :]], axis=1)
        vg = jnp.concatenate([vp_ref[:, g * d:(g + 1) * d], vc_ref[:, g * d:(g + 1) * d]], axis=0)
        vv = jnp.concatenate([vg, vg], axis=1)
        kt_lo = jnp.concatenate([kt, zeros_kt], axis=0)
        kt_hi = jnp.concatenate([zeros_kt, kt], axis=0)
        for pair in range(SW_GROUP // 2):
            h0 = g * SW_GROUP + 2 * pair
            q_pair = q_ref[:, h0 * d:(h0 + 2) * d]
            outs = []
            for sub, kt_sel in enumerate((kt_lo, kt_hi)):
                hd = h0 + sub
                logits = _dot(q_pair, kt_sel) + bias_ref[hd]
                sink = sink_ref[hd:hd + 1, 0:1]
                m = jnp.maximum(jnp.max(logits, axis=-1, keepdims=True), sink)
                pr = jnp.exp(logits - m)
                denom = jnp.sum(pr, axis=-1, keepdims=True) + jnp.exp(sink - m)
                outs.append(_dot(pr.astype(BF16), vv) / denom)
            o_ref[:, h0 * d:(h0 + 2) * d] = jnp.where(lane < d, outs[0], outs[1]).astype(BF16)


def _swa(sq, skt, sv, bias2, sink):
    b, s, _ = sq.shape
    W = WINDOW
    prev = lambda j: jnp.maximum(j - 1, 0)
    return pl.pallas_call(
        _swa_body,
        grid=(b, s // W),
        in_specs=[
            pl.BlockSpec((None, W, SW_Q_W), lambda i, j: (i, j, 0)),
            pl.BlockSpec((None, SW_KV_W, W), lambda i, j: (i, 0, prev(j))),
            pl.BlockSpec((None, SW_KV_W, W), lambda i, j: (i, 0, j)),
            pl.BlockSpec((None, W, SW_KV_W), lambda i, j: (i, prev(j), 0)),
            pl.BlockSpec((None, W, SW_KV_W), lambda i, j: (i, j, 0)),
            pl.BlockSpec((None, SW_Q_HEADS, W, 2 * W), lambda i, j: (jnp.minimum(j, 1), 0, 0, 0)),
            _const_spec((SW_Q_HEADS, V7X_LANES)),
        ],
        out_specs=pl.BlockSpec((None, W, SW_Q_W), lambda i, j: (i, j, 0)),
        out_shape=jax.ShapeDtypeStruct((b, s, SW_Q_W), BF16),
        compiler_params=_params("parallel", "arbitrary"),
        name="swa",
    )(sq, skt, skt, sv, sv, bias2, sink)


def _merge_body(x_ref, ha_ref, hb_ref, ga_ref, gb_ref, wa_ref, wb_ref, wo_ref, o_ref):
    ya = _dot(ha_ref[...], wa_ref[...])
    yb = _dot(hb_ref[...], wb_ref[...])
    mixed = _sigmoid(ga_ref[...].astype(F32)) * ya + _sigmoid(gb_ref[...].astype(F32)) * yb
    o_ref[...] = x_ref[...] + _dot(mixed.astype(BF16), wo_ref[...])


def _merge(x2, ha, hb, ga, gb, wa, wb, wo, tm):
    n_tok = x2.shape[0]
    tok = lambda w: pl.BlockSpec((tm, w), lambda i: (i, 0))
    return pl.pallas_call(
        _merge_body,
        grid=(n_tok // tm,),
        in_specs=[tok(D_MODEL), tok(ML_V_W), tok(SW_Q_W), tok(D_MODEL), tok(D_MODEL),
                  _const_spec(wa.shape), _const_spec(wb.shape), _const_spec(wo.shape)],
        out_specs=tok(D_MODEL),
        out_shape=jax.ShapeDtypeStruct((n_tok, D_MODEL), F32),
        compiler_params=_params("parallel"),
        name="merge",
    )(x2, ha, hb, ga, gb, wa, wb, wo)


def _ple_body(x_ref, p_ref, g_ref, wg_ref, wp_ref, o_ref):
    xf = x_ref[...]
    n = (xf * _rms_scale(xf) * g_ref[...]).astype(BF16)
    gate = _sigmoid(_dot(n, wg_ref[...]))
    o_ref[...] = xf + gate * _dot(p_ref[...].astype(BF16), wp_ref[...])


def _ple(x2, p2, g, wg, wp, tm):
    n_tok = x2.shape[0]
    return pl.pallas_call(
        _ple_body,
        grid=(n_tok // tm,),
        in_specs=[pl.BlockSpec((tm, D_MODEL), lambda i: (i, 0)), pl.BlockSpec((tm, P_DIM), lambda i: (i, 0)),
                  _const_spec(g.shape), _const_spec(wg.shape), _const_spec(wp.shape)],
        out_specs=pl.BlockSpec((tm, D_MODEL), lambda i: (i, 0)),
        out_shape=jax.ShapeDtypeStruct((n_tok, D_MODEL), F32),
        compiler_params=_params("parallel"),
        name="ple",
    )(x2, p2, g, wg, wp)


def _t5_bucket(dist):
    max_exact = REL_BUCKETS // 2
    d = np.maximum(dist, 0)
    large = max_exact + (np.log(np.maximum(d, 1) / max_exact) / np.log(REL_MAX_DIST / max_exact)
                         * (REL_BUCKETS - max_exact)).astype(np.int32)
    large = np.minimum(large, REL_BUCKETS - 1)
    return np.where(d < max_exact, d, large).astype(np.int32)


def _swa_bias(rel_bias):
    W = WINDOW
    dist = np.arange(W)[:, None] + W - np.arange(2 * W)[None, :]
    in_window = (dist >= 0) & (dist < W)
    bias = jnp.transpose(rel_bias.astype(F32)[_t5_bucket(dist)], (2, 0, 1))
    later = jnp.where(in_window[None], bias, -jnp.inf)
    first = jnp.where((in_window & (np.arange(2 * W) >= W)[None, :])[None], bias, -jnp.inf)
    return jnp.stack([first, later])


def kernel(x, p, ffn1_norm, ffn1_wi, ffn1_wo, mix_norm, w_in, b_igate, b_fgate, ml_out_norm, q_norm, k_norm, sinks, rel_bias, w_a, w_b, w_out, ffn2_norm, ffn2_wi, ffn2_wo, ple_norm, w_ple_gate, w_ple):
    B, S, _ = x.shape
    depth = p.shape[0]
    n_tok = B * S
    tm = min(TOKEN_TILE, S)
    offs = [0] + [int(o) for o in np.cumsum(SPLITS)]
    col = lambda w, a, b: w[:, offs[a]:offs[b]]

    bias2 = _swa_bias(rel_bias)
    head_of = np.arange(SW_Q_W + SW_KV_W) // SW_HEAD_DIM
    group_ones = jnp.asarray(head_of[:, None] == head_of[None, :], BF16)
    row = lambda v: v.reshape(1, -1).astype(F32)

    x2 = x.reshape(n_tok, D_MODEL)
    for i in range(depth):
        w = w_in[i]
        x2 = _ffn(x2, row(ffn1_norm[i]), ffn1_wi[i].astype(BF16), ffn1_wo[i].astype(BF16), tm)
        wm = jnp.concatenate([col(w, 0, 1), col(w, 2, 4)], axis=1).astype(BF16)
        wkt = col(w, 1, 2).T.astype(BF16)
        wift = col(w, 4, 6).T.astype(BF16)
        ws = col(w, 6, 9).astype(BF16)
        wg = col(w, 9, 11).astype(BF16)
        qk_gain = jnp.concatenate([jnp.tile(q_norm[i], SW_Q_HEADS) * (SW_HEAD_DIM ** -0.5),
                                   jnp.tile(k_norm[i], SW_KV_HEADS)]).reshape(1, -1).astype(F32)
        mq, mkt, mv, mo, gt, sq, skt, sv, ga, gb = _inproj(
            x2.reshape(B, S, D_MODEL), row(mix_norm[i]), wm, wkt, wift, ws, wg, qk_gain, group_ones, tm)
        gate_bias = jnp.concatenate([b_igate[i], b_fgate[i]]).reshape(-1, 1).astype(F32)
        ha = _mlstm(mq, mkt, mv, mo, gt, gate_bias, ml_out_norm[i].astype(F32))
        sink = jnp.broadcast_to(sinks[i].astype(F32)[:, None], (SW_Q_HEADS, V7X_LANES))
        hb = _swa(sq, skt, sv, bias2, sink)
        flat = lambda a: a.reshape(n_tok, a.shape[-1])
        x2 = _merge(x2, flat(ha), flat(hb), flat(ga), flat(gb),
                    w_a[i].astype(BF16), w_b[i].astype(BF16), w_out[i].astype(BF16), tm)
        x2 = _ffn(x2, row(ffn2_norm[i]), ffn2_wi[i].astype(BF16), ffn2_wo[i].astype(BF16), tm)
        x2 = _ple(x2, p[i].reshape(n_tok, P_DIM), row(ple_norm[i]), w_ple_gate[i].astype(BF16),
                  w_ple[i].astype(BF16), tm)
    return x2.reshape(B, S, D_MODEL)
```

```python
import functools
import math

import jax
import jax.numpy as jnp
import numpy as np
from jax.experimental import pallas as pl
from jax.experimental.pallas import tpu as pltpu

D_MODEL = 1024
P_DIM = 256
D_FF = 2816
ML_HEADS = 4
ML_QK_DIM = 64
ML_V_DIM = 128
SW_Q_HEADS = 8
SW_KV_HEADS = 2
SW_HEAD_DIM = 64
WINDOW = 128
REL_BUCKETS = 32
REL_MAX_DIST = 128
EPS = 1e-6

ML_QK_W = ML_HEADS * ML_QK_DIM
ML_V_W = ML_HEADS * ML_V_DIM
SW_Q_W = SW_Q_HEADS * SW_HEAD_DIM
SW_KV_W = SW_KV_HEADS * SW_HEAD_DIM
SW_GROUP = SW_Q_HEADS // SW_KV_HEADS
SPLITS = (ML_QK_W, ML_QK_W, ML_V_W, ML_V_W, ML_HEADS, ML_HEADS, SW_Q_W, SW_KV_W, SW_KV_W, D_MODEL, D_MODEL)

V7X_LANES = 128
V7X_VMEM_LIMIT_BYTES = 56 * 1024 * 1024

FFN_CHUNK = 256
ML_CHUNK = 128
TOKEN_TILE = 512

F32 = jnp.float32
BF16 = jnp.bfloat16


def _dot(a, b):
    return jnp.dot(a, b, preferred_element_type=F32)


def lax_rsqrt(v):
    return jax.lax.rsqrt(v)


def _rms_scale(xf):
    return lax_rsqrt(jnp.mean(xf * xf, axis=-1, keepdims=True) + EPS)


def _sigmoid(v):
    return 1.0 / (1.0 + jnp.exp(-v))


def _const_spec(shape):
    nd = len(shape)
    return pl.BlockSpec(shape, lambda *_: (0,) * nd, pipeline_mode=pl.Buffered(1))


def _params(*sem):
    return pltpu.CompilerParams(dimension_semantics=sem, vmem_limit_bytes=V7X_VMEM_LIMIT_BYTES)


def _ffn_body(x_ref, g_ref, wi_ref, wo_ref, o_ref, n_ref, act_ref):
    xf = x_ref[...]
    n_ref[...] = (xf * _rms_scale(xf) * g_ref[...]).astype(BF16)
    for j in range(D_FF // FFN_CHUNK):
        lo = j * FFN_CHUNK
        hg = _dot(n_ref[...], wi_ref[:, lo:lo + FFN_CHUNK])
        hu = _dot(n_ref[...], wi_ref[:, D_FF + lo:D_FF + lo + FFN_CHUNK])
        act_ref[:, lo:lo + FFN_CHUNK] = (hg * _sigmoid(hg) * hu).astype(BF16)
    o_ref[...] = x_ref[...] + 0.5 * _dot(act_ref[...], wo_ref[...])


def _ffn(x2, g, wi, wo, tm):
    n_tok = x2.shape[0]
    return pl.pallas_call(
        _ffn_body,
        grid=(n_tok // tm,),
        in_specs=[
            pl.BlockSpec((tm, D_MODEL), lambda i: (i, 0)),
            _const_spec((1, D_MODEL)),
            _const_spec((D_MODEL, 2 * D_FF)),
            _const_spec((D_FF, D_MODEL)),
        ],
        out_specs=pl.BlockSpec((tm, D_MODEL), lambda i: (i, 0)),
        out_shape=jax.ShapeDtypeStruct((n_tok, D_MODEL), F32),
        scratch_shapes=[pltpu.VMEM((tm, D_MODEL), BF16), pltpu.VMEM((tm, D_FF), BF16)],
        compiler_params=_params("parallel"),
        name="ffn",
    )(x2, g, wi, wo)


def _inproj_body(x_ref, g_ref, wm_ref, wkt_ref, wift_ref, ws_ref, wg_ref, gain_ref, ones_ref,
                 mq_ref, mkt_ref, mv_ref, mo_ref, gt_ref, sq_ref, skt_ref, sv_ref, ga_ref, gb_ref):
    xf = x_ref[...]
    u = (xf * _rms_scale(xf) * g_ref[...]).astype(BF16)
    zm = _dot(u, wm_ref[...])
    mq_ref[...] = zm[:, :ML_QK_W].astype(BF16)
    mv_ref[...] = zm[:, ML_QK_W:ML_QK_W + ML_V_W].astype(BF16)
    mo_ref[...] = zm[:, ML_QK_W + ML_V_W:].astype(BF16)
    nt = (((1,), (1,)), ((), ()))
    mkt_ref[...] = (jax.lax.dot_general(wkt_ref[...], u, nt, preferred_element_type=F32)
                    * (1.0 / math.sqrt(ML_QK_DIM))).astype(BF16)
    gt_ref[...] = jax.lax.dot_general(wift_ref[...], u, nt, preferred_element_type=F32)
    zs = _dot(u, ws_ref[...])
    qk = zs[:, :SW_Q_W + SW_KV_W]
    ssq = _dot((qk * qk).astype(BF16), ones_ref[...])
    qkn = qk * lax_rsqrt(ssq * (1.0 / SW_HEAD_DIM) + EPS) * gain_ref[...]
    sq_ref[...] = qkn[:, :SW_Q_W].astype(BF16)
    skt_ref[...] = qkn[:, SW_Q_W:].T.astype(BF16)
    sv_ref[...] = zs[:, SW_Q_W + SW_KV_W:].astype(BF16)
    zg = _dot(u, wg_ref[...])
    ga_ref[...] = zg[:, :D_MODEL].astype(BF16)
    gb_ref[...] = zg[:, D_MODEL:].astype(BF16)


def _inproj(x3, g, wm, wkt, wift, ws, wg, gain, ones, tm):
    b, s, _ = x3.shape
    tok = lambda w: pl.BlockSpec((None, tm, w), lambda i, j: (i, j, 0))
    tok_t = lambda w: pl.BlockSpec((None, w, tm), lambda i, j: (i, 0, j))
    sd = jax.ShapeDtypeStruct
    return pl.pallas_call(
        _inproj_body,
        grid=(b, s // tm),
        in_specs=[tok(D_MODEL)] + [_const_spec(a.shape) for a in (g, wm, wkt, wift, ws, wg, gain, ones)],
        out_specs=[tok(ML_QK_W), tok_t(ML_QK_W), tok(ML_V_W), tok(ML_V_W), tok_t(2 * ML_HEADS),
                   tok(SW_Q_W), tok_t(SW_KV_W), tok(SW_KV_W), tok(D_MODEL), tok(D_MODEL)],
        out_shape=[sd((b, s, ML_QK_W), BF16), sd((b, ML_QK_W, s), BF16), sd((b, s, ML_V_W), BF16),
                   sd((b, s, ML_V_W), BF16), sd((b, 2 * ML_HEADS, s), F32),
                   sd((b, s, SW_Q_W), BF16), sd((b, SW_KV_W, s), BF16), sd((b, s, SW_KV_W), BF16),
                   sd((b, s, D_MODEL), BF16), sd((b, s, D_MODEL), BF16)],
        compiler_params=_params("parallel", "parallel"),
        name="inproj",
    )(x3, g, wm, wkt, wift, ws, wg, gain, ones)


def _mlstm_body(q_ref, kt_ref, v_ref, o_ref, gt_ref, bias_ref, gain_ref, h_ref, c_ref, m_ref):
    L = ML_CHUNK

    @pl.when(pl.program_id(1) == 0)
    def _():
        c_ref[...] = jnp.zeros_like(c_ref)
        m_ref[...] = jnp.zeros_like(m_ref)

    pre = gt_ref[...] + bias_ref[...]
    logf8 = jnp.minimum(pre, 0.0) - jnp.log1p(jnp.exp(-jnp.abs(pre)))
    lane = jax.lax.broadcasted_iota(jnp.int32, (2 * ML_HEADS, L), 1)
    b8 = logf8
    k = 1
    while k < L:
        b8 = b8 + jnp.where(lane >= k, pltpu.roll(b8, k, 1), 0.0)
        k *= 2

    H, dk, dv = ML_HEADS, ML_QK_DIM, ML_V_DIM
    t_idx = jax.lax.broadcasted_iota(jnp.int32, (L, H * L), 0)
    s_idx = jax.lax.broadcasted_iota(jnp.int32, (L, H * L), 1)
    causal = (s_idx & (L - 1)) <= t_idx
    ones_col = (jax.lax.broadcasted_iota(jnp.int32, (L, V7X_LANES), 1) == 0).astype(BF16)

    m_prev, g_tot, a_row, m_new, c_rows, f_rows = [], [], [], [], [], []
    for h in range(H):
        b_row = b8[H + h:H + h + 1, :]
        c_row = pre[h:h + 1, :] - b_row
        g_tot.append(b_row[:, L - 1:L])
        a_row.append(g_tot[h] + c_row)
        m_prev.append(m_ref[h:h + 1, 0:1])
        m_new.append(jnp.maximum(g_tot[h] + m_prev[h], jnp.max(a_row[h], axis=-1, keepdims=True)))
        c_rows.append(c_row)
        f_rows.append(logf8[H + h:H + h + 1, :])
    dm = jnp.where(causal, jnp.concatenate(c_rows, axis=1), -jnp.inf)
    bm = jnp.where(causal, jnp.concatenate(f_rows, axis=1), 0.0)
    m_col = [jnp.maximum(jnp.max(dm[:, h * L:(h + 1) * L], axis=-1, keepdims=True), m_prev[h]) for h in range(H)]
    b_col = [jnp.sum(bm[:, h * L:(h + 1) * L], axis=-1, keepdims=True) for h in range(H)]
    m_all = jnp.concatenate([jnp.broadcast_to(m_col[h], (L, L)) for h in range(H)], axis=1)

    r_blk = jax.lax.broadcasted_iota(jnp.int32, (H * dk, H * L), 0) // dk
    c_blk = jax.lax.broadcasted_iota(jnp.int32, (H * dk, H * L), 1) // L
    kt_all = kt_ref[...]
    k_bd = jnp.where(r_blk == c_blk, jnp.concatenate([kt_all] * H, axis=1), jnp.zeros((), BF16))
    w_all = (jnp.exp(dm - m_all) * _dot(q_ref[...], k_bd)).astype(BF16)

    zeros_c = jnp.zeros((dk, 2 * dv), BF16)
    zeros_k = jnp.zeros((dk, 2 * dk), BF16)
    for h in range(H):
        pair = h // 2
        v_aug = jnp.concatenate([v_ref[:, h * dv:(h + 1) * dv], ones_col], axis=1)
        c_aug = c_ref[h]
        c_bf = c_aug.astype(BF16)
        rhs = jnp.concatenate([v_aug] + ([c_bf, zeros_c] if h % 2 == 0 else [zeros_c, c_bf]), axis=0)
        s_inter = jnp.exp(m_prev[h] - m_col[h])
        q_s = (q_ref[:, pair * 2 * dk:(pair + 1) * 2 * dk].astype(F32) * s_inter).astype(BF16)
        kt = kt_ref[h * dk:(h + 1) * dk, :]
        kw = (kt.astype(F32) * jnp.exp(a_row[h] - m_new[h])).astype(BF16)
        lhs = jnp.concatenate([jnp.concatenate([w_all[:, h * L:(h + 1) * L], q_s], axis=1),
                               jnp.concatenate([kw, zeros_k], axis=1)], axis=0)
        res = _dot(lhs, rhs)
        den = res[:L, dv:dv + 1]
        inv = 1.0 / jnp.maximum(jnp.abs(den), jnp.exp(-(b_col[h] + m_col[h])))
        hh = res[:L, :dv] * inv
        hn = hh * lax_rsqrt(jnp.mean(hh * hh, axis=-1, keepdims=True) + EPS) * gain_ref[h:h + 1, :]
        gate = _sigmoid(o_ref[:, h * dv:(h + 1) * dv].astype(F32))
        h_ref[:, h * dv:(h + 1) * dv] = (gate * hn).astype(BF16)
        c_ref[h] = jnp.exp(g_tot[h] + m_prev[h] - m_new[h]) * c_aug + res[L:, :]
        m_ref[h:h + 1, :] = jnp.broadcast_to(m_new[h], (1, V7X_LANES))


def _mlstm(mq, mkt, mv, mo, gt, bias, gain):
    b, s, _ = mq.shape
    L = ML_CHUNK
    return pl.pallas_call(
        _mlstm_body,
        grid=(b, s // L),
        in_specs=[
            pl.BlockSpec((None, L, ML_QK_W), lambda i, j: (i, j, 0)),
            pl.BlockSpec((None, ML_QK_W, L), lambda i, j: (i, 0, j)),
            pl.BlockSpec((None, L, ML_V_W), lambda i, j: (i, j, 0)),
            pl.BlockSpec((None, L, ML_V_W), lambda i, j: (i, j, 0)),
            pl.BlockSpec((None, 2 * ML_HEADS, L), lambda i, j: (i, 0, j)),
            _const_spec((2 * ML_HEADS, 1)),
            _const_spec((ML_HEADS, ML_V_DIM)),
        ],
        out_specs=pl.BlockSpec((None, L, ML_V_W), lambda i, j: (i, j, 0)),
        out_shape=jax.ShapeDtypeStruct((b, s, ML_V_W), BF16),
        scratch_shapes=[pltpu.VMEM((ML_HEADS, ML_QK_DIM, 2 * ML_V_DIM), F32),
                        pltpu.VMEM((2 * ML_HEADS, V7X_LANES), F32)],
        compiler_params=_params("parallel", "arbitrary"),
        name="mlstm",
    )(mq, mkt, mv, mo, gt, bias, gain)


def _swa_body(q_ref, ktp_ref, ktc_ref, vp_ref, vc_ref, bias_ref, sink_ref, o_ref):
    W = WINDOW
    d = SW_HEAD_DIM
    pairs = SW_GROUP // 2
    lane = jax.lax.broadcasted_iota(jnp.int32, (pairs * W, 2 * d), 1)
    zeros_kt = jnp.zeros((d, 2 * W), BF16)
    for g in range(SW_KV_HEADS):
        kt = jnp.concatenate([ktp_ref[g * d:(g + 1) * d, :], ktc_ref[g * d:(g + 1) * d, :]], axis=1)
        vg = jnp.concatenate([vp_ref[:, g * d:(g + 1) * d], vc_ref[:, g * d:(g + 1) * d]], axis=0)
        vv = jnp.concatenate([vg, vg], axis=1)
        kt_sel = jnp.concatenate([jnp.concatenate([kt, zeros_kt], axis=0),
                                  jnp.concatenate([zeros_kt, kt], axis=0)], axis=1)
        q_rows = jnp.concatenate([q_ref[:, (g * pairs + pr) * 2 * d:(g * pairs + pr + 1) * 2 * d]
                                  for pr in range(pairs)], axis=0)
        scores = _dot(q_rows, kt_sel)
        outs = []
        for sub in range(2):
            logits = scores[:, sub * 2 * W:(sub + 1) * 2 * W] + bias_ref[g, sub]
            sink = jnp.concatenate(
                [jnp.broadcast_to(sink_ref[g * SW_GROUP + 2 * pr + sub:g * SW_GROUP + 2 * pr + sub + 1, 0:1], (W, 1))
                 for pr in range(pairs)], axis=0)
            m = jnp.maximum(jnp.max(logits, axis=-1, keepdims=True), sink)
            pr_ = jnp.exp(logits - m)
            denom = jnp.sum(pr_, axis=-1, keepdims=True) + jnp.exp(sink - m)
            outs.append(_dot(pr_.astype(BF16), vv) / denom)
        sel = jnp.where(lane < d, outs[0], outs[1]).astype(BF16)
        for pr in range(pairs):
            o_ref[:, (g * pairs + pr) * 2 * d:(g * pairs + pr + 1) * 2 * d] = sel[pr * W:(pr + 1) * W]


def _swa(sq, skt, sv, bias2, sink):
    b, s, _ = sq.shape
    W = WINDOW
    prev = lambda j: jnp.maximum(j - 1, 0)
    return pl.pallas_call(
        _swa_body,
        grid=(b, s // W),
        in_specs=[
            pl.BlockSpec((None, W, SW_Q_W), lambda i, j: (i, j, 0)),
            pl.BlockSpec((None, SW_KV_W, W), lambda i, j: (i, 0, prev(j))),
            pl.BlockSpec((None, SW_KV_W, W), lambda i, j: (i, 0, j)),
            pl.BlockSpec((None, W, SW_KV_W), lambda i, j: (i, prev(j), 0)),
            pl.BlockSpec((None, W, SW_KV_W), lambda i, j: (i, j, 0)),
            pl.BlockSpec((None, SW_KV_HEADS, 2, (SW_GROUP // 2) * W, 2 * W),
                         lambda i, j: (jnp.minimum(j, 1), 0, 0, 0, 0)),
            _const_spec((SW_Q_HEADS, V7X_LANES)),
        ],
        out_specs=pl.BlockSpec((None, W, SW_Q_W), lambda i, j: (i, j, 0)),
        out_shape=jax.ShapeDtypeStruct((b, s, SW_Q_W), BF16),
        compiler_params=_params("parallel", "arbitrary"),
        name="swa",
    )(sq, skt, skt, sv, sv, bias2, sink)


def _merge_body(x_ref, ha_ref, hb_ref, ga_ref, gb_ref, wa_ref, wb_ref, wo_ref, o_ref):
    ya = _dot(ha_ref[...], wa_ref[...])
    yb = _dot(hb_ref[...], wb_ref[...])
    mixed = _sigmoid(ga_ref[...].astype(F32)) * ya + _sigmoid(gb_ref[...].astype(F32)) * yb
    o_ref[...] = x_ref[...] + _dot(mixed.astype(BF16), wo_ref[...])


def _merge(x2, ha, hb, ga, gb, wa, wb, wo, tm):
    n_tok = x2.shape[0]
    tok = lambda w: pl.BlockSpec((tm, w), lambda i: (i, 0))
    return pl.pallas_call(
        _merge_body,
        grid=(n_tok // tm,),
        in_specs=[tok(D_MODEL), tok(ML_V_W), tok(SW_Q_W), tok(D_MODEL), tok(D_MODEL),
                  _const_spec(wa.shape), _const_spec(wb.shape), _const_spec(wo.shape)],
        out_specs=tok(D_MODEL),
        out_shape=jax.ShapeDtypeStruct((n_tok, D_MODEL), F32),
        compiler_params=_params("parallel"),
        name="merge",
    )(x2, ha, hb, ga, gb, wa, wb, wo)


def _ple_body(x_ref, p_ref, g_ref, wg_ref, wp_ref, o_ref):
    xf = x_ref[...]
    n = (xf * _rms_scale(xf) * g_ref[...]).astype(BF16)
    gate = _sigmoid(_dot(n, wg_ref[...]))
    o_ref[...] = xf + gate * _dot(p_ref[...].astype(BF16), wp_ref[...])


def _ple(x2, p2, g, wg, wp, tm):
    n_tok = x2.shape[0]
    return pl.pallas_call(
        _ple_body,
        grid=(n_tok // tm,),
        in_specs=[pl.BlockSpec((tm, D_MODEL), lambda i: (i, 0)), pl.BlockSpec((tm, P_DIM), lambda i: (i, 0)),
                  _const_spec(g.shape), _const_spec(wg.shape), _const_spec(wp.shape)],
        out_specs=pl.BlockSpec((tm, D_MODEL), lambda i: (i, 0)),
        out_shape=jax.ShapeDtypeStruct((n_tok, D_MODEL), F32),
        compiler_params=_params("parallel"),
        name="ple",
    )(x2, p2, g, wg, wp)


def _t5_bucket(dist):
    max_exact = REL_BUCKETS // 2
    d = np.maximum(dist, 0)
    large = max_exact + (np.log(np.maximum(d, 1) / max_exact) / np.log(REL_MAX_DIST / max_exact)
                         * (REL_BUCKETS - max_exact)).astype(np.int32)
    large = np.minimum(large, REL_BUCKETS - 1)
    return np.where(d < max_exact, d, large).astype(np.int32)


def _swa_bias(rel_bias):
    W = WINDOW
    pairs = SW_GROUP // 2
    dist = np.arange(W)[:, None] + W - np.arange(2 * W)[None, :]
    in_window = (dist >= 0) & (dist < W)
    onehot = (_t5_bucket(dist).reshape(-1)[None, :] == np.arange(REL_BUCKETS)[:, None]).astype(np.float32)
    bias = jnp.dot(rel_bias.astype(F32).T, onehot, precision=jax.lax.Precision.HIGHEST)
    bias = bias.reshape(SW_KV_HEADS, pairs, 2, W, 2 * W)
    bias = jnp.transpose(bias, (0, 2, 1, 3, 4)).reshape(SW_KV_HEADS, 2, pairs * W, 2 * W)
    mask_later = np.tile(in_window, (pairs, 1))
    mask_first = np.tile(in_window & (np.arange(2 * W) >= W)[None, :], (pairs, 1))
    return jnp.stack([jnp.where(mask_first, bias, -jnp.inf), jnp.where(mask_later, bias, -jnp.inf)])


def kernel(x, p, ffn1_norm, ffn1_wi, ffn1_wo, mix_norm, w_in, b_igate, b_fgate, ml_out_norm, q_norm, k_norm, sinks, rel_bias, w_a, w_b, w_out, ffn2_norm, ffn2_wi, ffn2_wo, ple_norm, w_ple_gate, w_ple):
    B, S, _ = x.shape
    depth = p.shape[0]
    n_tok = B * S
    tm = min(TOKEN_TILE, S)
    offs = [0] + [int(o) for o in np.cumsum(SPLITS)]
    col = lambda w, a, b: w[:, offs[a]:offs[b]]

    bias2 = _swa_bias(rel_bias)
    head_of = np.arange(SW_Q_W + SW_KV_W) // SW_HEAD_DIM
    group_ones = jnp.asarray(head_of[:, None] == head_of[None, :], BF16)
    row = lambda v: v.reshape(1, -1).astype(F32)

    x2 = x.reshape(n_tok, D_MODEL)
    for i in range(depth):
        w = w_in[i]
        x2 = _ffn(x2, row(ffn1_norm[i]), ffn1_wi[i].astype(BF16), ffn1_wo[i].astype(BF16), tm)
        wm = jnp.concatenate([col(w, 0, 1), col(w, 2, 4)], axis=1).astype(BF16)
        wkt = col(w, 1, 2).T.astype(BF16)
        wift = col(w, 4, 6).T.astype(BF16)
        ws = col(w, 6, 9).astype(BF16)
        wg = col(w, 9, 11).astype(BF16)
        qk_gain = jnp.concatenate([jnp.tile(q_norm[i], SW_Q_HEADS) * (SW_HEAD_DIM ** -0.5),
                                   jnp.tile(k_norm[i], SW_KV_HEADS)]).reshape(1, -1).astype(F32)
        mq, mkt, mv, mo, gt, sq, skt, sv, ga, gb = _inproj(
            x2.reshape(B, S, D_MODEL), row(mix_norm[i]), wm, wkt, wift, ws, wg, qk_gain, group_ones, tm)
        gate_bias = jnp.concatenate([b_igate[i], b_fgate[i]]).reshape(-1, 1).astype(F32)
        ha = _mlstm(mq, mkt, mv, mo, gt, gate_bias, ml_out_norm[i].astype(F32))
        sink = jnp.broadcast_to(sinks[i].astype(F32)[:, None], (SW_Q_HEADS, V7X_LANES))
        hb = _swa(sq, skt, sv, bias2, sink)
        flat = lambda a: a.reshape(n_tok, a.shape[-1])
        x2 = _merge(x2, flat(ha), flat(hb), flat(ga), flat(gb),
                    w_a[i].astype(BF16), w_b[i].astype(BF16), w_out[i].astype(BF16), tm)
        x2 = _ffn(x2, row(ffn2_norm[i]), ffn2_wi[i].astype(BF16), ffn2_wo[i].astype(BF16), tm)
        x2 = _ple(x2, p[i].reshape(n_tok, P_DIM), row(ple_norm[i]), w_ple_gate[i].astype(BF16),
                  w_ple[i].astype(BF16), tm)
    return x2.reshape(B, S, D_MODEL)
```

```python
import functools
import math

import jax
import jax.numpy as jnp
import numpy as np
from jax.experimental import pallas as pl
from jax.experimental.pallas import tpu as pltpu

D_MODEL = 1024
P_DIM = 256
D_FF = 2816
ML_HEADS = 4
ML_QK_DIM = 64
ML_V_DIM = 128
SW_Q_HEADS = 8
SW_KV_HEADS = 2
SW_HEAD_DIM = 64
WINDOW = 128
REL_BUCKETS = 32
REL_MAX_DIST = 128
EPS = 1e-6

ML_QK_W = ML_HEADS * ML_QK_DIM
ML_V_W = ML_HEADS * ML_V_DIM
SW_Q_W = SW_Q_HEADS * SW_HEAD_DIM
SW_KV_W = SW_KV_HEADS * SW_HEAD_DIM
SW_GROUP = SW_Q_HEADS // SW_KV_HEADS
SPLITS = (ML_QK_W, ML_QK_W, ML_V_W, ML_V_W, ML_HEADS, ML_HEADS, SW_Q_W, SW_KV_W, SW_KV_W, D_MODEL, D_MODEL)

V7X_LANES = 128
V7X_VMEM_LIMIT_BYTES = 56 * 1024 * 1024

FFN_CHUNK = 256
ML_CHUNK = 128
ML_ROWS = 2
TOKEN_TILE = 512

F32 = jnp.float32
BF16 = jnp.bfloat16


def _dot(a, b):
    return jnp.dot(a, b, preferred_element_type=F32)


def lax_rsqrt(v):
    return jax.lax.rsqrt(v)


def _rms_scale(xf):
    return lax_rsqrt(jnp.mean(xf * xf, axis=-1, keepdims=True) + EPS)


def _sigmoid(v):
    return 1.0 / (1.0 + jnp.exp(-v))


def _const_spec(shape):
    nd = len(shape)
    return pl.BlockSpec(shape, lambda *_: (0,) * nd, pipeline_mode=pl.Buffered(1))


def _params(*sem, flags=None):
    return pltpu.CompilerParams(dimension_semantics=sem, vmem_limit_bytes=V7X_VMEM_LIMIT_BYTES, flags=flags)


def _ffn_body(x_ref, g_ref, wi_ref, wo_ref, o_ref, n_ref, act_ref):
    xf = x_ref[...]
    n_ref[...] = (xf * _rms_scale(xf) * g_ref[...]).astype(BF16)
    for j in range(D_FF // FFN_CHUNK):
        lo = j * FFN_CHUNK
        hg = _dot(n_ref[...], wi_ref[:, lo:lo + FFN_CHUNK])
        hu = _dot(n_ref[...], wi_ref[:, D_FF + lo:D_FF + lo + FFN_CHUNK])
        act_ref[:, lo:lo + FFN_CHUNK] = (hg * _sigmoid(hg) * hu).astype(BF16)
    o_ref[...] = x_ref[...] + 0.5 * _dot(act_ref[...], wo_ref[...])


def _ffn(x2, g, wi, wo, tm):
    n_tok = x2.shape[0]
    return pl.pallas_call(
        _ffn_body,
        grid=(n_tok // tm,),
        in_specs=[
            pl.BlockSpec((tm, D_MODEL), lambda i: (i, 0)),
            _const_spec((1, D_MODEL)),
            _const_spec((D_MODEL, 2 * D_FF)),
            _const_spec((D_FF, D_MODEL)),
        ],
        out_specs=pl.BlockSpec((tm, D_MODEL), lambda i: (i, 0)),
        out_shape=jax.ShapeDtypeStruct((n_tok, D_MODEL), F32),
        scratch_shapes=[pltpu.VMEM((tm, D_MODEL), BF16), pltpu.VMEM((tm, D_FF), BF16)],
        compiler_params=_params("parallel"),
        name="ffn",
    )(x2, g, wi, wo)


def _inproj_body(x_ref, g_ref, wm_ref, wkt_ref, wift_ref, bif_ref, ws_ref, wg_ref, gain_ref, ones_ref,
                 mq_ref, mkt_ref, mv_ref, mo_ref, gl_ref, gc_ref, sq_ref, skt_ref, sv_ref, ga_ref, gb_ref):
    xf = x_ref[...]
    u = (xf * _rms_scale(xf) * g_ref[...]).astype(BF16)
    zm = _dot(u, wm_ref[...])
    mq_ref[...] = zm[:, :ML_QK_W].astype(BF16)
    mv_ref[...] = zm[:, ML_QK_W:ML_QK_W + ML_V_W].astype(BF16)
    mo_ref[...] = zm[:, ML_QK_W + ML_V_W:].astype(BF16)
    nt = (((1,), (1,)), ((), ()))
    mkt_ref[...] = (jax.lax.dot_general(wkt_ref[...], u, nt, preferred_element_type=F32)
                    * (1.0 / math.sqrt(ML_QK_DIM))).astype(BF16)
    pre = jax.lax.dot_general(wift_ref[...], u, nt, preferred_element_type=F32) + bif_ref[...]
    logf = jnp.minimum(pre, 0.0) - jnp.log1p(jnp.exp(-jnp.abs(pre)))
    pos = jax.lax.broadcasted_iota(jnp.int32, pre.shape, 1) & (ML_CHUNK - 1)
    csum = logf
    k = 1
    while k < ML_CHUNK:
        csum = csum + jnp.where(pos >= k, pltpu.roll(csum, k, 1), 0.0)
        k *= 2
    gl_ref[...] = jnp.where(jax.lax.broadcasted_iota(jnp.int32, pre.shape, 0) < ML_HEADS, pre, logf)
    gc_ref[...] = csum
    zs = _dot(u, ws_ref[...])
    qk = zs[:, :SW_Q_W + SW_KV_W]
    ssq = _dot((qk * qk).astype(BF16), ones_ref[...])
    qkn = qk * lax_rsqrt(ssq * (1.0 / SW_HEAD_DIM) + EPS) * gain_ref[...]
    sq_ref[...] = qkn[:, :SW_Q_W].astype(BF16)
    skt_ref[...] = qkn[:, SW_Q_W:].T.astype(BF16)
    sv_ref[...] = zs[:, SW_Q_W + SW_KV_W:].astype(BF16)
    zg = _dot(u, wg_ref[...])
    ga_ref[...] = zg[:, :D_MODEL].astype(BF16)
    gb_ref[...] = zg[:, D_MODEL:].astype(BF16)


def _inproj(x3, g, wm, wkt, wift, bif, ws, wg, gain, ones, tm):
    b, s, _ = x3.shape
    assert tm % ML_CHUNK == 0
    tok = lambda w: pl.BlockSpec((None, tm, w), lambda i, j: (i, j, 0))
    tok_t = lambda w: pl.BlockSpec((None, w, tm), lambda i, j: (i, 0, j))
    sd = jax.ShapeDtypeStruct
    return pl.pallas_call(
        _inproj_body,
        grid=(b, s // tm),
        in_specs=[tok(D_MODEL)] + [_const_spec(a.shape) for a in (g, wm, wkt, wift, bif, ws, wg, gain, ones)],
        out_specs=[tok(ML_QK_W), tok_t(ML_QK_W), tok(ML_V_W), tok(ML_V_W), tok_t(2 * ML_HEADS), tok_t(2 * ML_HEADS),
                   tok(SW_Q_W), tok_t(SW_KV_W), tok(SW_KV_W), tok(D_MODEL), tok(D_MODEL)],
        out_shape=[sd((b, s, ML_QK_W), BF16), sd((b, ML_QK_W, s), BF16), sd((b, s, ML_V_W), BF16),
                   sd((b, s, ML_V_W), BF16), sd((b, 2 * ML_HEADS, s), F32), sd((b, 2 * ML_HEADS, s), F32),
                   sd((b, s, SW_Q_W), BF16), sd((b, SW_KV_W, s), BF16), sd((b, s, SW_KV_W), BF16),
                   sd((b, s, D_MODEL), BF16), sd((b, s, D_MODEL), BF16)],
        compiler_params=_params("parallel", "parallel"),
        name="inproj",
    )(x3, g, wm, wkt, wift, bif, ws, wg, gain, ones)


def _mlstm_body(q_ref, kt_ref, v_ref, o_ref, gl_ref, gc_ref, gain_ref, h_ref, c_ref, m_ref):
    L = ML_CHUNK
    H, dk, dv = ML_HEADS, ML_QK_DIM, ML_V_DIM

    @pl.when(pl.program_id(1) == 0)
    def _():
        c_ref[...] = jnp.zeros_like(c_ref)
        m_ref[...] = jnp.zeros_like(m_ref)

    t_idx = jax.lax.broadcasted_iota(jnp.int32, (L, H * L), 0)
    s_idx = jax.lax.broadcasted_iota(jnp.int32, (L, H * L), 1)
    causal = (s_idx & (L - 1)) <= t_idx
    ones_col = (jax.lax.broadcasted_iota(jnp.int32, (L, V7X_LANES), 1) == 0).astype(BF16)
    r_blk = jax.lax.broadcasted_iota(jnp.int32, (H * dk, H * L), 0) // dk
    c_blk = jax.lax.broadcasted_iota(jnp.int32, (H * dk, H * L), 1) // L
    zeros_c = jnp.zeros((dk, 2 * dv), BF16)
    zeros_k = jnp.zeros((dk, 2 * dk), BF16)

    rows = range(q_ref.shape[0])
    heads = range(H)
    m_prev = [[m_ref[r, h:h + 1, 0:1] for h in heads] for r in rows]
    c_aug = [[c_ref[r, h] for h in heads] for r in rows]

    g_tot, a_row, m_new, dm, bm = [], [], [], [], []
    for r in rows:
        gl = gl_ref[r]
        gc = gc_ref[r]
        b_rows = [gc[H + h:H + h + 1, :] for h in heads]
        c_rows = [gl[h:h + 1, :] - b_rows[h] for h in heads]
        g_tot.append([b_rows[h][:, L - 1:L] for h in heads])
        a_row.append([g_tot[r][h] + c_rows[h] for h in heads])
        m_new.append([jnp.maximum(g_tot[r][h] + m_prev[r][h], jnp.max(a_row[r][h], axis=-1, keepdims=True))
                      for h in heads])
        dm.append(jnp.where(causal, jnp.concatenate(c_rows, axis=1), -jnp.inf))
        bm.append(jnp.where(causal, jnp.concatenate([gl[H + h:H + h + 1, :] for h in heads], axis=1), 0.0))

    s_all = []
    for r in rows:
        kt_all = kt_ref[r]
        k_bd = jnp.where(r_blk == c_blk, jnp.concatenate([kt_all] * H, axis=1), jnp.zeros((), BF16))
        s_all.append(_dot(q_ref[r], k_bd))

    m_col = [[jnp.maximum(jnp.max(dm[r][:, h * L:(h + 1) * L], axis=-1, keepdims=True), m_prev[r][h])
              for h in heads] for r in rows]
    b_col = [[jnp.sum(bm[r][:, h * L:(h + 1) * L], axis=-1, keepdims=True) for h in heads] for r in rows]
    w_all = []
    for r in rows:
        m_all = jnp.concatenate([jnp.broadcast_to(m_col[r][h], (L, L)) for h in heads], axis=1)
        w_all.append((jnp.exp(dm[r] - m_all) * s_all[r]).astype(BF16))

    res = [[None] * H for _ in rows]
    for r in rows:
        for h in heads:
            pair = h // 2
            v_aug = jnp.concatenate([v_ref[r, :, h * dv:(h + 1) * dv], ones_col], axis=1)
            c_bf = c_aug[r][h].astype(BF16)
            rhs = jnp.concatenate([v_aug] + ([c_bf, zeros_c] if h % 2 == 0 else [zeros_c, c_bf]), axis=0)
            s_inter = jnp.exp(m_prev[r][h] - m_col[r][h])
            q_s = (q_ref[r, :, pair * 2 * dk:(pair + 1) * 2 * dk].astype(F32) * s_inter).astype(BF16)
            kt = kt_ref[r, h * dk:(h + 1) * dk, :]
            kw = (kt.astype(F32) * jnp.exp(a_row[r][h] - m_new[r][h])).astype(BF16)
            lhs = jnp.concatenate([jnp.concatenate([w_all[r][:, h * L:(h + 1) * L], q_s], axis=1),
                                   jnp.concatenate([kw, zeros_k], axis=1)], axis=0)
            res[r][h] = _dot(lhs, rhs)

    for r in rows:
        for h in heads:
            c_ref[r, h] = jnp.exp(g_tot[r][h] + m_prev[r][h] - m_new[r][h]) * c_aug[r][h] + res[r][h][L:, :]
            m_ref[r, h:h + 1, :] = jnp.broadcast_to(m_new[r][h], (1, V7X_LANES))
    for r in rows:
        for h in heads:
            den = res[r][h][:L, dv:dv + 1]
            inv = 1.0 / jnp.maximum(jnp.abs(den), jnp.exp(-(b_col[r][h] + m_col[r][h])))
            hh = res[r][h][:L, :dv] * inv
            hn = hh * lax_rsqrt(jnp.mean(hh * hh, axis=-1, keepdims=True) + EPS) * gain_ref[h:h + 1, :]
            gate = _sigmoid(o_ref[r, :, h * dv:(h + 1) * dv].astype(F32))
            h_ref[r, :, h * dv:(h + 1) * dv] = (gate * hn).astype(BF16)


def _mlstm(mq, mkt, mv, mo, gl, gc, gain):
    b, s, _ = mq.shape
    L = ML_CHUNK
    R = ML_ROWS if b % ML_ROWS == 0 else 1
    return pl.pallas_call(
        _mlstm_body,
        grid=(b // R, s // L),
        in_specs=[
            pl.BlockSpec((R, L, ML_QK_W), lambda i, j: (i, j, 0)),
            pl.BlockSpec((R, ML_QK_W, L), lambda i, j: (i, 0, j)),
            pl.BlockSpec((R, L, ML_V_W), lambda i, j: (i, j, 0)),
            pl.BlockSpec((R, L, ML_V_W), lambda i, j: (i, j, 0)),
            pl.BlockSpec((R, 2 * ML_HEADS, L), lambda i, j: (i, 0, j)),
            pl.BlockSpec((R, 2 * ML_HEADS, L), lambda i, j: (i, 0, j)),
            _const_spec((ML_HEADS, ML_V_DIM)),
        ],
        out_specs=pl.BlockSpec((R, L, ML_V_W), lambda i, j: (i, j, 0)),
        out_shape=jax.ShapeDtypeStruct((b, s, ML_V_W), BF16),
        scratch_shapes=[pltpu.VMEM((R, ML_HEADS, ML_QK_DIM, 2 * ML_V_DIM), F32),
                        pltpu.VMEM((R, 2 * ML_HEADS, V7X_LANES), F32)],
        compiler_params=_params("parallel", "arbitrary"),
        name="mlstm",
    )(mq, mkt, mv, mo, gl, gc, gain)


def _swa_body(q_ref, ktp_ref, ktc_ref, vp_ref, vc_ref, bias_ref, sink_ref, o_ref):
    W = WINDOW
    d = SW_HEAD_DIM
    pairs = SW_GROUP // 2
    rows = pairs * W
    groups = range(SW_KV_HEADS)
    lane = jax.lax.broadcasted_iota(jnp.int32, (rows, 2 * d), 1)
    zeros_kt = jnp.zeros((d, 2 * W), BF16)
    ones_v = jnp.ones((2 * W, 2 * d), BF16)

    scores, vv = [], []
    for g in groups:
        kt = jnp.concatenate([ktp_ref[g * d:(g + 1) * d, :], ktc_ref[g * d:(g + 1) * d, :]], axis=1)
        vg = jnp.concatenate([vp_ref[:, g * d:(g + 1) * d], vc_ref[:, g * d:(g + 1) * d]], axis=0)
        vv.append(jnp.concatenate([vg, vg, ones_v], axis=1))
        kt_sel = jnp.concatenate([jnp.concatenate([kt, zeros_kt], axis=0),
                                  jnp.concatenate([zeros_kt, kt], axis=0)], axis=1)
        q_rows = jnp.concatenate([q_ref[:, (g * pairs + pr) * 2 * d:(g * pairs + pr + 1) * 2 * d]
                                  for pr in range(pairs)], axis=0)
        scores.append(_dot(q_rows, kt_sel))

    probs, tail = [], []
    for g in groups:
        for sub in range(2):
            logits = scores[g][:, sub * 2 * W:(sub + 1) * 2 * W] + bias_ref[g, sub]
            sink = jnp.concatenate(
                [jnp.broadcast_to(sink_ref[g * SW_GROUP + 2 * pr + sub:g * SW_GROUP + 2 * pr + sub + 1, :],
                                  (W, V7X_LANES)) for pr in range(pairs)], axis=0)
            m = jnp.maximum(jnp.broadcast_to(jnp.max(logits, axis=-1, keepdims=True), (rows, V7X_LANES)), sink)
            probs.append(jnp.exp(logits - jnp.concatenate([m, m], axis=1)).astype(BF16))
            tail.append(jnp.exp(sink - m))

    res = [_dot(probs[2 * g + sub], vv[g]) for g in groups for sub in range(2)]
    for g in groups:
        outs = [res[2 * g + sub][:, :2 * d] / (res[2 * g + sub][:, 2 * d:] + tail[2 * g + sub]) for sub in range(2)]
        sel = jnp.where(lane < d, outs[0], outs[1]).astype(BF16)
        for pr in range(pairs):
            o_ref[:, (g * pairs + pr) * 2 * d:(g * pairs + pr + 1) * 2 * d] = sel[pr * W:(pr + 1) * W]


def _swa(sq, skt, sv, bias2, sink):
    b, s, _ = sq.shape
    W = WINDOW
    prev = lambda j: jnp.maximum(j - 1, 0)
    return pl.pallas_call(
        _swa_body,
        grid=(b, s // W),
        in_specs=[
            pl.BlockSpec((None, W, SW_Q_W), lambda i, j: (i, j, 0)),
            pl.BlockSpec((None, SW_KV_W, W), lambda i, j: (i, 0, prev(j))),
            pl.BlockSpec((None, SW_KV_W, W), lambda i, j: (i, 0, j)),
            pl.BlockSpec((None, W, SW_KV_W), lambda i, j: (i, prev(j), 0)),
            pl.BlockSpec((None, W, SW_KV_W), lambda i, j: (i, j, 0)),
            pl.BlockSpec((None, SW_KV_HEADS, 2, (SW_GROUP // 2) * W, 2 * W),
                         lambda i, j: (jnp.minimum(j, 1), 0, 0, 0, 0)),
            _const_spec((SW_Q_HEADS, V7X_LANES)),
        ],
        out_specs=pl.BlockSpec((None, W, SW_Q_W), lambda i, j: (i, j, 0)),
        out_shape=jax.ShapeDtypeStruct((b, s, SW_Q_W), BF16),
        compiler_params=_params("parallel", "arbitrary"),
        name="swa",
    )(sq, skt, skt, sv, sv, bias2, sink)


def _merge_body(x_ref, ha_ref, hb_ref, ga_ref, gb_ref, wa_ref, wb_ref, wo_ref, o_ref):
    ya = _dot(ha_ref[...], wa_ref[...])
    yb = _dot(hb_ref[...], wb_ref[...])
    mixed = _sigmoid(ga_ref[...].astype(F32)) * ya + _sigmoid(gb_ref[...].astype(F32)) * yb
    o_ref[...] = x_ref[...] + _dot(mixed.astype(BF16), wo_ref[...])


def _merge(x2, ha, hb, ga, gb, wa, wb, wo, tm):
    n_tok = x2.shape[0]
    tok = lambda w: pl.BlockSpec((tm, w), lambda i: (i, 0))
    return pl.pallas_call(
        _merge_body,
        grid=(n_tok // tm,),
        in_specs=[tok(D_MODEL), tok(ML_V_W), tok(SW_Q_W), tok(D_MODEL), tok(D_MODEL),
                  _const_spec(wa.shape), _const_spec(wb.shape), _const_spec(wo.shape)],
        out_specs=tok(D_MODEL),
        out_shape=jax.ShapeDtypeStruct((n_tok, D_MODEL), F32),
        compiler_params=_params("parallel"),
        name="merge",
    )(x2, ha, hb, ga, gb, wa, wb, wo)


def _ple_body(x_ref, p_ref, g_ref, wg_ref, wp_ref, o_ref):
    xf = x_ref[...]
    n = (xf * _rms_scale(xf) * g_ref[...]).astype(BF16)
    gate = _sigmoid(_dot(n, wg_ref[...]))
    o_ref[...] = xf + gate * _dot(p_ref[...].astype(BF16), wp_ref[...])


def _ple(x2, p2, g, wg, wp, tm):
    n_tok = x2.shape[0]
    return pl.pallas_call(
        _ple_body,
        grid=(n_tok // tm,),
        in_specs=[pl.BlockSpec((tm, D_MODEL), lambda i: (i, 0)), pl.BlockSpec((tm, P_DIM), lambda i: (i, 0)),
                  _const_spec(g.shape), _const_spec(wg.shape), _const_spec(wp.shape)],
        out_specs=pl.BlockSpec((tm, D_MODEL), lambda i: (i, 0)),
        out_shape=jax.ShapeDtypeStruct((n_tok, D_MODEL), F32),
        compiler_params=_params("parallel"),
        name="ple",
    )(x2, p2, g, wg, wp)


def _t5_bucket(dist):
    max_exact = REL_BUCKETS // 2
    d = np.maximum(dist, 0)
    large = max_exact + (np.log(np.maximum(d, 1) / max_exact) / np.log(REL_MAX_DIST / max_exact)
                         * (REL_BUCKETS - max_exact)).astype(np.int32)
    large = np.minimum(large, REL_BUCKETS - 1)
    return np.where(d < max_exact, d, large).astype(np.int32)


def _swa_bias(rel_bias):
    W = WINDOW
    pairs = SW_GROUP // 2
    dist = np.arange(W)[:, None] + W - np.arange(2 * W)[None, :]
    in_window = (dist >= 0) & (dist < W)
    onehot = (_t5_bucket(dist).reshape(-1)[None, :] == np.arange(REL_BUCKETS)[:, None]).astype(np.float32)
    bias = jnp.dot(rel_bias.astype(F32).T, onehot, precision=jax.lax.Precision.HIGHEST)
    bias = bias.reshape(SW_KV_HEADS, pairs, 2, W, 2 * W)
    bias = jnp.transpose(bias, (0, 2, 1, 3, 4)).reshape(SW_KV_HEADS, 2, pairs * W, 2 * W)
    mask_later = np.tile(in_window, (pairs, 1))
    mask_first = np.tile(in_window & (np.arange(2 * W) >= W)[None, :], (pairs, 1))
    return jnp.stack([jnp.where(mask_first, bias, -jnp.inf), jnp.where(mask_later, bias, -jnp.inf)])


def kernel(x, p, ffn1_norm, ffn1_wi, ffn1_wo, mix_norm, w_in, b_igate, b_fgate, ml_out_norm, q_norm, k_norm, sinks, rel_bias, w_a, w_b, w_out, ffn2_norm, ffn2_wi, ffn2_wo, ple_norm, w_ple_gate, w_ple):
    B, S, _ = x.shape
    depth = p.shape[0]
    n_tok = B * S
    tm = min(TOKEN_TILE, S)
    offs = [0] + [int(o) for o in np.cumsum(SPLITS)]
    col = lambda w, a, b: w[:, offs[a]:offs[b]]

    bias2 = _swa_bias(rel_bias)
    head_of = np.arange(SW_Q_W + SW_KV_W) // SW_HEAD_DIM
    group_ones = jnp.asarray(head_of[:, None] == head_of[None, :], BF16)
    row = lambda v: v.reshape(1, -1).astype(F32)

    x2 = x.reshape(n_tok, D_MODEL)
    for i in range(depth):
        w = w_in[i]
        x2 = _ffn(x2, row(ffn1_norm[i]), ffn1_wi[i].astype(BF16), ffn1_wo[i].astype(BF16), tm)
        wm = jnp.concatenate([col(w, 0, 1), col(w, 2, 4)], axis=1).astype(BF16)
        wkt = col(w, 1, 2).T.astype(BF16)
        wift = col(w, 4, 6).T.astype(BF16)
        ws = col(w, 6, 9).astype(BF16)
        wg = col(w, 9, 11).astype(BF16)
        qk_gain = jnp.concatenate([jnp.tile(q_norm[i], SW_Q_HEADS) * (SW_HEAD_DIM ** -0.5),
                                   jnp.tile(k_norm[i], SW_KV_HEADS)]).reshape(1, -1).astype(F32)
        gate_bias = jnp.concatenate([b_igate[i], b_fgate[i]]).reshape(-1, 1).astype(F32)
        mq, mkt, mv, mo, gl, gc, sq, skt, sv, ga, gb = _inproj(
            x2.reshape(B, S, D_MODEL), row(mix_norm[i]), wm, wkt, wift, gate_bias, ws, wg, qk_gain, group_ones, tm)
        ha = _mlstm(mq, mkt, mv, mo, gl, gc, ml_out_norm[i].astype(F32))
        sink = jnp.broadcast_to(sinks[i].astype(F32)[:, None], (SW_Q_HEADS, V7X_LANES))
        hb = _swa(sq, skt, sv, bias2, sink)
        flat = lambda a: a.reshape(n_tok, a.shape[-1])
        x2 = _merge(x2, flat(ha), flat(hb), flat(ga), flat(gb),
                    w_a[i].astype(BF16), w_b[i].astype(BF16), w_out[i].astype(BF16), tm)
        x2 = _ffn(x2, row(ffn2_norm[i]), ffn2_wi[i].astype(BF16), ffn2_wo[i].astype(BF16), tm)
        x2 = _ple(x2, p[i].reshape(n_tok, P_DIM), row(ple_norm[i]), w_ple_gate[i].astype(BF16),
                  w_ple[i].astype(BF16), tm)
    return x2.reshape(B, S, D_MODEL)
```

```python
import functools
import math

import jax
import jax.numpy as jnp
import numpy as np
from jax.experimental import pallas as pl
from jax.experimental.pallas import tpu as pltpu

D_MODEL = 1024
P_DIM = 256
D_FF = 2816
ML_HEADS = 4
ML_QK_DIM = 64
ML_V_DIM = 128
SW_Q_HEADS = 8
SW_KV_HEADS = 2
SW_HEAD_DIM = 64
WINDOW = 128
REL_BUCKETS = 32
REL_MAX_DIST = 128
EPS = 1e-6

ML_QK_W = ML_HEADS * ML_QK_DIM
ML_V_W = ML_HEADS * ML_V_DIM
SW_Q_W = SW_Q_HEADS * SW_HEAD_DIM
SW_KV_W = SW_KV_HEADS * SW_HEAD_DIM
SW_GROUP = SW_Q_HEADS // SW_KV_HEADS
SPLITS = (ML_QK_W, ML_QK_W, ML_V_W, ML_V_W, ML_HEADS, ML_HEADS, SW_Q_W, SW_KV_W, SW_KV_W, D_MODEL, D_MODEL)

V7X_LANES = 128
V7X_VMEM_LIMIT_BYTES = 56 * 1024 * 1024

FFN_CHUNK = 256
ML_CHUNK = 128
ML_ROWS = 2
TOKEN_TILE = 512

F32 = jnp.float32
BF16 = jnp.bfloat16


def _dot(a, b):
    return jnp.dot(a, b, preferred_element_type=F32)


def lax_rsqrt(v):
    return jax.lax.rsqrt(v)


def _rms_scale(xf):
    return lax_rsqrt(jnp.mean(xf * xf, axis=-1, keepdims=True) + EPS)


def _sigmoid(v):
    return 1.0 / (1.0 + jnp.exp(-v))


def _const_spec(shape):
    nd = len(shape)
    return pl.BlockSpec(shape, lambda *_: (0,) * nd, pipeline_mode=pl.Buffered(1))


def _layer_spec(stacked, layer):
    tail = stacked.shape[1:]
    return pl.BlockSpec((None,) + tail, lambda *_: (layer,) + (0,) * len(tail), pipeline_mode=pl.Buffered(1))


def _params(*sem, flags=None):
    return pltpu.CompilerParams(dimension_semantics=sem, vmem_limit_bytes=V7X_VMEM_LIMIT_BYTES, flags=flags)


def _ffn_half_step(xf, g_ref, wi_ref, wo_ref, n_ref, act_ref):
    n_ref[...] = (xf * _rms_scale(xf) * g_ref[...]).astype(BF16)
    for j in range(D_FF // FFN_CHUNK):
        lo = j * FFN_CHUNK
        hg = _dot(n_ref[...], wi_ref[:, lo:lo + FFN_CHUNK])
        hu = _dot(n_ref[...], wi_ref[:, D_FF + lo:D_FF + lo + FFN_CHUNK])
        act_ref[:, lo:lo + FFN_CHUNK] = (hg * _sigmoid(hg) * hu).astype(BF16)
    return xf + 0.5 * _dot(act_ref[...], wo_ref[...])


def _ffn_body(x_ref, g_ref, wi_ref, wo_ref, o_ref, n_ref, act_ref):
    o_ref[...] = _ffn_half_step(x_ref[...], g_ref, wi_ref, wo_ref, n_ref, act_ref)


def _ffn(x2, g, wi, wo, layer, tm):
    n_tok = x2.shape[0]
    return pl.pallas_call(
        _ffn_body,
        grid=(n_tok // tm,),
        in_specs=[pl.BlockSpec((tm, D_MODEL), lambda i: (i, 0)),
                  _layer_spec(g, layer), _layer_spec(wi, layer), _layer_spec(wo, layer)],
        out_specs=pl.BlockSpec((tm, D_MODEL), lambda i: (i, 0)),
        out_shape=jax.ShapeDtypeStruct((n_tok, D_MODEL), F32),
        scratch_shapes=[pltpu.VMEM((tm, D_MODEL), BF16), pltpu.VMEM((tm, D_FF), BF16)],
        compiler_params=_params("parallel"),
        name="ffn",
    )(x2, g, wi, wo)


def _inproj_body(x_ref, g_ref, wm_ref, wkt_ref, wift_ref, bif_ref, ws_ref, wg_ref, gain_ref, ones_ref,
                 mq_ref, mkt_ref, mv_ref, mo_ref, gl_ref, gc_ref, sq_ref, skt_ref, sv_ref, ga_ref, gb_ref):
    xf = x_ref[...]
    u = (xf * _rms_scale(xf) * g_ref[...]).astype(BF16)
    zm = _dot(u, wm_ref[...])
    mq_ref[...] = zm[:, :ML_QK_W].astype(BF16)
    mv_ref[...] = zm[:, ML_QK_W:ML_QK_W + ML_V_W].astype(BF16)
    mo_ref[...] = zm[:, ML_QK_W + ML_V_W:].astype(BF16)
    nt = (((1,), (1,)), ((), ()))
    mkt_ref[...] = (jax.lax.dot_general(wkt_ref[...], u, nt, preferred_element_type=F32)
                    * (1.0 / math.sqrt(ML_QK_DIM))).astype(BF16)
    pre = jax.lax.dot_general(wift_ref[...], u, nt, preferred_element_type=F32) + bif_ref[...]
    logf = jnp.minimum(pre, 0.0) - jnp.log1p(jnp.exp(-jnp.abs(pre)))
    pos = jax.lax.broadcasted_iota(jnp.int32, pre.shape, 1) & (ML_CHUNK - 1)
    csum = logf
    k = 1
    while k < ML_CHUNK:
        csum = csum + jnp.where(pos >= k, pltpu.roll(csum, k, 1), 0.0)
        k *= 2
    gl_ref[...] = jnp.where(jax.lax.broadcasted_iota(jnp.int32, pre.shape, 0) < ML_HEADS, pre, logf)
    gc_ref[...] = csum
    zs = _dot(u, ws_ref[...])
    qk = zs[:, :SW_Q_W + SW_KV_W]
    ssq = _dot((qk * qk).astype(BF16), ones_ref[...])
    qkn = qk * lax_rsqrt(ssq * (1.0 / SW_HEAD_DIM) + EPS) * gain_ref[...]
    sq_ref[...] = qkn[:, :SW_Q_W].astype(BF16)
    skt_ref[...] = qkn[:, SW_Q_W:].T.astype(BF16)
    sv_ref[...] = zs[:, SW_Q_W + SW_KV_W:].astype(BF16)
    zg = _dot(u, wg_ref[...])
    ga_ref[...] = zg[:, :D_MODEL].astype(BF16)
    gb_ref[...] = zg[:, D_MODEL:].astype(BF16)


def _inproj(x3, g, wm, wkt, wift, bif, ws, wg, gain, ones, layer, tm):
    b, s, _ = x3.shape
    assert tm % ML_CHUNK == 0
    tok = lambda w: pl.BlockSpec((None, tm, w), lambda i, j: (i, j, 0))
    tok_t = lambda w: pl.BlockSpec((None, w, tm), lambda i, j: (i, 0, j))
    sd = jax.ShapeDtypeStruct
    return pl.pallas_call(
        _inproj_body,
        grid=(b, s // tm),
        in_specs=[tok(D_MODEL)] + [_layer_spec(a, layer) for a in (g, wm, wkt, wift, bif, ws, wg, gain)]
                 + [_const_spec(ones.shape)],
        out_specs=[tok(ML_QK_W), tok_t(ML_QK_W), tok(ML_V_W), tok(ML_V_W), tok_t(2 * ML_HEADS), tok_t(2 * ML_HEADS),
                   tok(SW_Q_W), tok_t(SW_KV_W), tok(SW_KV_W), tok(D_MODEL), tok(D_MODEL)],
        out_shape=[sd((b, s, ML_QK_W), BF16), sd((b, ML_QK_W, s), BF16), sd((b, s, ML_V_W), BF16),
                   sd((b, s, ML_V_W), BF16), sd((b, 2 * ML_HEADS, s), F32), sd((b, 2 * ML_HEADS, s), F32),
                   sd((b, s, SW_Q_W), BF16), sd((b, SW_KV_W, s), BF16), sd((b, s, SW_KV_W), BF16),
                   sd((b, s, D_MODEL), BF16), sd((b, s, D_MODEL), BF16)],
        compiler_params=_params("parallel", "parallel"),
        name="inproj",
    )(x3, g, wm, wkt, wift, bif, ws, wg, gain, ones)


def _mlstm_body(q_ref, kt_ref, v_ref, o_ref, gl_ref, gc_ref, gain_ref, h_ref, c_ref, m_ref):
    L = ML_CHUNK
    H, dk, dv = ML_HEADS, ML_QK_DIM, ML_V_DIM

    @pl.when(pl.program_id(1) == 0)
    def _():
        c_ref[...] = jnp.zeros_like(c_ref)
        m_ref[...] = jnp.zeros_like(m_ref)

    t_idx = jax.lax.broadcasted_iota(jnp.int32, (L, H * L), 0)
    s_idx = jax.lax.broadcasted_iota(jnp.int32, (L, H * L), 1)
    causal = (s_idx & (L - 1)) <= t_idx
    ones_col = (jax.lax.broadcasted_iota(jnp.int32, (L, V7X_LANES), 1) == 0).astype(BF16)
    r_blk = jax.lax.broadcasted_iota(jnp.int32, (H * dk, H * L), 0) // dk
    c_blk = jax.lax.broadcasted_iota(jnp.int32, (H * dk, H * L), 1) // L
    zeros_c = jnp.zeros((dk, 2 * dv), BF16)
    zeros_k = jnp.zeros((dk, 2 * dk), BF16)

    rows = range(q_ref.shape[0])
    heads = range(H)
    m_prev = [[m_ref[r, h:h + 1, 0:1] for h in heads] for r in rows]
    c_aug = [[c_ref[r, h] for h in heads] for r in rows]

    g_tot, a_row, m_new, dm, bm = [], [], [], [], []
    for r in rows:
        gl = gl_ref[r]
        gc = gc_ref[r]
        b_rows = [gc[H + h:H + h + 1, :] for h in heads]
        c_rows = [gl[h:h + 1, :] - b_rows[h] for h in heads]
        g_tot.append([b_rows[h][:, L - 1:L] for h in heads])
        a_row.append([g_tot[r][h] + c_rows[h] for h in heads])
        m_new.append([jnp.maximum(g_tot[r][h] + m_prev[r][h], jnp.max(a_row[r][h], axis=-1, keepdims=True))
                      for h in heads])
        dm.append(jnp.where(causal, jnp.concatenate(c_rows, axis=1), -jnp.inf))
        bm.append(jnp.where(causal, jnp.concatenate([gl[H + h:H + h + 1, :] for h in heads], axis=1), 0.0))

    s_all = []
    for r in rows:
        kt_all = kt_ref[r]
        k_bd = jnp.where(r_blk == c_blk, jnp.concatenate([kt_all] * H, axis=1), jnp.zeros((), BF16))
        s_all.append(_dot(q_ref[r], k_bd))

    m_col = [[jnp.maximum(jnp.max(dm[r][:, h * L:(h + 1) * L], axis=-1, keepdims=True), m_prev[r][h])
              for h in heads] for r in rows]
    b_col = [[jnp.sum(bm[r][:, h * L:(h + 1) * L], axis=-1, keepdims=True) for h in heads] for r in rows]
    w_all = []
    for r in rows:
        m_all = jnp.concatenate([jnp.broadcast_to(m_col[r][h], (L, L)) for h in heads], axis=1)
        w_all.append((jnp.exp(dm[r] - m_all) * s_all[r]).astype(BF16))

    res = [[None] * H for _ in rows]
    for r in rows:
        for h in heads:
            pair = h // 2
            v_aug = jnp.concatenate([v_ref[r, :, h * dv:(h + 1) * dv], ones_col], axis=1)
            c_bf = c_aug[r][h].astype(BF16)
            rhs = jnp.concatenate([v_aug] + ([c_bf, zeros_c] if h % 2 == 0 else [zeros_c, c_bf]), axis=0)
            s_inter = jnp.exp(m_prev[r][h] - m_col[r][h])
            q_s = (q_ref[r, :, pair * 2 * dk:(pair + 1) * 2 * dk].astype(F32) * s_inter).astype(BF16)
            kt = kt_ref[r, h * dk:(h + 1) * dk, :]
            kw = (kt.astype(F32) * jnp.exp(a_row[r][h] - m_new[r][h])).astype(BF16)
            lhs = jnp.concatenate([jnp.concatenate([w_all[r][:, h * L:(h + 1) * L], q_s], axis=1),
                                   jnp.concatenate([kw, zeros_k], axis=1)], axis=0)
            res[r][h] = _dot(lhs, rhs)

    for r in rows:
        for h in heads:
            c_ref[r, h] = jnp.exp(g_tot[r][h] + m_prev[r][h] - m_new[r][h]) * c_aug[r][h] + res[r][h][L:, :]
            m_ref[r, h:h + 1, :] = jnp.broadcast_to(m_new[r][h], (1, V7X_LANES))
    for r in rows:
        for h in heads:
            den = res[r][h][:L, dv:dv + 1]
            inv = 1.0 / jnp.maximum(jnp.abs(den), jnp.exp(-(b_col[r][h] + m_col[r][h])))
            hh = res[r][h][:L, :dv] * inv
            hn = hh * lax_rsqrt(jnp.mean(hh * hh, axis=-1, keepdims=True) + EPS) * gain_ref[h:h + 1, :]
            gate = _sigmoid(o_ref[r, :, h * dv:(h + 1) * dv].astype(F32))
            h_ref[r, :, h * dv:(h + 1) * dv] = (gate * hn).astype(BF16)


def _mlstm(mq, mkt, mv, mo, gl, gc, gain, layer):
    b, s, _ = mq.shape
    L = ML_CHUNK
    R = ML_ROWS if b % ML_ROWS == 0 else 1
    return pl.pallas_call(
        _mlstm_body,
        grid=(b // R, s // L),
        in_specs=[
            pl.BlockSpec((R, L, ML_QK_W), lambda i, j: (i, j, 0)),
            pl.BlockSpec((R, ML_QK_W, L), lambda i, j: (i, 0, j)),
            pl.BlockSpec((R, L, ML_V_W), lambda i, j: (i, j, 0)),
            pl.BlockSpec((R, L, ML_V_W), lambda i, j: (i, j, 0)),
            pl.BlockSpec((R, 2 * ML_HEADS, L), lambda i, j: (i, 0, j)),
            pl.BlockSpec((R, 2 * ML_HEADS, L), lambda i, j: (i, 0, j)),
            _layer_spec(gain, layer),
        ],
        out_specs=pl.BlockSpec((R, L, ML_V_W), lambda i, j: (i, j, 0)),
        out_shape=jax.ShapeDtypeStruct((b, s, ML_V_W), BF16),
        scratch_shapes=[pltpu.VMEM((R, ML_HEADS, ML_QK_DIM, 2 * ML_V_DIM), F32),
                        pltpu.VMEM((R, 2 * ML_HEADS, V7X_LANES), F32)],
        compiler_params=_params("parallel", "arbitrary"),
        name="mlstm",
    )(mq, mkt, mv, mo, gl, gc, gain)


def _swa_body(q_ref, ktp_ref, ktc_ref, vp_ref, vc_ref, bias_ref, sink_ref, o_ref):
    W = WINDOW
    d = SW_HEAD_DIM
    pairs = SW_GROUP // 2
    rows = pairs * W
    groups = range(SW_KV_HEADS)
    lane = jax.lax.broadcasted_iota(jnp.int32, (rows, 2 * d), 1)
    zeros_kt = jnp.zeros((d, 2 * W), BF16)
    ones_v = jnp.ones((2 * W, 2 * d), BF16)

    scores, vv = [], []
    for g in groups:
        kt = jnp.concatenate([ktp_ref[g * d:(g + 1) * d, :], ktc_ref[g * d:(g + 1) * d, :]], axis=1)
        vg = jnp.concatenate([vp_ref[:, g * d:(g + 1) * d], vc_ref[:, g * d:(g + 1) * d]], axis=0)
        vv.append(jnp.concatenate([vg, vg, ones_v], axis=1))
        kt_sel = jnp.concatenate([jnp.concatenate([kt, zeros_kt], axis=0),
                                  jnp.concatenate([zeros_kt, kt], axis=0)], axis=1)
        q_rows = jnp.concatenate([q_ref[:, (g * pairs + pr) * 2 * d:(g * pairs + pr + 1) * 2 * d]
                                  for pr in range(pairs)], axis=0)
        scores.append(_dot(q_rows, kt_sel))

    probs, tail = [], []
    for g in groups:
        for sub in range(2):
            logits = scores[g][:, sub * 2 * W:(sub + 1) * 2 * W] + bias_ref[g, sub]
            sink = jnp.concatenate(
                [jnp.broadcast_to(sink_ref[g * SW_GROUP + 2 * pr + sub:g * SW_GROUP + 2 * pr + sub + 1, :],
                                  (W, V7X_LANES)) for pr in range(pairs)], axis=0)
            m = jnp.maximum(jnp.broadcast_to(jnp.max(logits, axis=-1, keepdims=True), (rows, V7X_LANES)), sink)
            probs.append(jnp.exp(logits - jnp.concatenate([m, m], axis=1)).astype(BF16))
            tail.append(jnp.exp(sink - m))

    res = [_dot(probs[2 * g + sub], vv[g]) for g in groups for sub in range(2)]
    for g in groups:
        outs = [res[2 * g + sub][:, :2 * d] / (res[2 * g + sub][:, 2 * d:] + tail[2 * g + sub]) for sub in range(2)]
        sel = jnp.where(lane < d, outs[0], outs[1]).astype(BF16)
        for pr in range(pairs):
            o_ref[:, (g * pairs + pr) * 2 * d:(g * pairs + pr + 1) * 2 * d] = sel[pr * W:(pr + 1) * W]


def _swa(sq, skt, sv, bias2, sink, layer):
    b, s, _ = sq.shape
    W = WINDOW
    prev = lambda j: jnp.maximum(j - 1, 0)
    return pl.pallas_call(
        _swa_body,
        grid=(b, s // W),
        in_specs=[
            pl.BlockSpec((None, W, SW_Q_W), lambda i, j: (i, j, 0)),
            pl.BlockSpec((None, SW_KV_W, W), lambda i, j: (i, 0, prev(j))),
            pl.BlockSpec((None, SW_KV_W, W), lambda i, j: (i, 0, j)),
            pl.BlockSpec((None, W, SW_KV_W), lambda i, j: (i, prev(j), 0)),
            pl.BlockSpec((None, W, SW_KV_W), lambda i, j: (i, j, 0)),
            pl.BlockSpec((None, SW_KV_HEADS, 2, (SW_GROUP // 2) * W, 2 * W),
                         lambda i, j: (jnp.minimum(j, 1), 0, 0, 0, 0)),
            _layer_spec(sink, layer),
        ],
        out_specs=pl.BlockSpec((None, W, SW_Q_W), lambda i, j: (i, j, 0)),
        out_shape=jax.ShapeDtypeStruct((b, s, SW_Q_W), BF16),
        compiler_params=_params("parallel", "arbitrary"),
        name="swa",
    )(sq, skt, skt, sv, sv, bias2, sink)


def _post_body(x_ref, ha_ref, hb_ref, ga_ref, gb_ref, p_ref, wa_ref, wb_ref, wo_ref,
               fg_ref, fwi_ref, fwo_ref, pg_ref, pwg_ref, pwp_ref, o_ref, n_ref, act_ref):
    ya = _dot(ha_ref[...], wa_ref[...])
    yb = _dot(hb_ref[...], wb_ref[...])
    mixed = _sigmoid(ga_ref[...].astype(F32)) * ya + _sigmoid(gb_ref[...].astype(F32)) * yb
    x1 = x_ref[...] + _dot(mixed.astype(BF16), wo_ref[...])
    x2 = _ffn_half_step(x1, fg_ref, fwi_ref, fwo_ref, n_ref, act_ref)
    n_ref[...] = (x2 * _rms_scale(x2) * pg_ref[...]).astype(BF16)
    gate = _sigmoid(_dot(n_ref[...], pwg_ref[...]))
    o_ref[...] = x2 + gate * _dot(p_ref[...].astype(BF16), pwp_ref[...])


def _post(x2, ha, hb, ga, gb, p3, wa, wb, wo, fg, fwi, fwo, pg, pwg, pwp, layer, tm):
    n_tok = x2.shape[0]
    tok = lambda w: pl.BlockSpec((tm, w), lambda i: (i, 0))
    weights = (wa, wb, wo, fg, fwi, fwo, pg, pwg, pwp)
    return pl.pallas_call(
        _post_body,
        grid=(n_tok // tm,),
        in_specs=[tok(D_MODEL), tok(ML_V_W), tok(SW_Q_W), tok(D_MODEL), tok(D_MODEL),
                  pl.BlockSpec((None, tm, P_DIM), lambda i: (layer, i, 0))]
                 + [_layer_spec(w, layer) for w in weights],
        out_specs=tok(D_MODEL),
        out_shape=jax.ShapeDtypeStruct((n_tok, D_MODEL), F32),
        scratch_shapes=[pltpu.VMEM((tm, D_MODEL), BF16), pltpu.VMEM((tm, D_FF), BF16)],
        compiler_params=_params("parallel"),
        name="post",
    )(x2, ha, hb, ga, gb, p3, *weights)


def _t5_bucket(dist):
    max_exact = REL_BUCKETS // 2
    d = np.maximum(dist, 0)
    large = max_exact + (np.log(np.maximum(d, 1) / max_exact) / np.log(REL_MAX_DIST / max_exact)
                         * (REL_BUCKETS - max_exact)).astype(np.int32)
    large = np.minimum(large, REL_BUCKETS - 1)
    return np.where(d < max_exact, d, large).astype(np.int32)


def _swa_bias(rel_bias):
    W = WINDOW
    pairs = SW_GROUP // 2
    dist = np.arange(W)[:, None] + W - np.arange(2 * W)[None, :]
    in_window = (dist >= 0) & (dist < W)
    onehot = (_t5_bucket(dist).reshape(-1)[None, :] == np.arange(REL_BUCKETS)[:, None]).astype(np.float32)
    bias = jnp.dot(rel_bias.astype(F32).T, onehot, precision=jax.lax.Precision.HIGHEST)
    bias = bias.reshape(SW_KV_HEADS, pairs, 2, W, 2 * W)
    bias = jnp.transpose(bias, (0, 2, 1, 3, 4)).reshape(SW_KV_HEADS, 2, pairs * W, 2 * W)
    mask_later = np.tile(in_window, (pairs, 1))
    mask_first = np.tile(in_window & (np.arange(2 * W) >= W)[None, :], (pairs, 1))
    return jnp.stack([jnp.where(mask_first, bias, -jnp.inf), jnp.where(mask_later, bias, -jnp.inf)])


def kernel(x, p, ffn1_norm, ffn1_wi, ffn1_wo, mix_norm, w_in, b_igate, b_fgate, ml_out_norm, q_norm, k_norm, sinks, rel_bias, w_a, w_b, w_out, ffn2_norm, ffn2_wi, ffn2_wo, ple_norm, w_ple_gate, w_ple):
    B, S, _ = x.shape
    depth = p.shape[0]
    n_tok = B * S
    tm = min(TOKEN_TILE, S)
    offs = [0] + [int(o) for o in np.cumsum(SPLITS)]

    bf = lambda w: w.astype(BF16)
    rows = lambda v: v.reshape(depth, 1, -1).astype(F32)
    w_in_b = bf(w_in)
    col = lambda a, b: w_in_b[:, :, offs[a]:offs[b]]
    wm = jnp.concatenate([col(0, 1), col(2, 4)], axis=2)
    wkt = jnp.swapaxes(col(1, 2), 1, 2)
    wift = jnp.swapaxes(col(4, 6), 1, 2)
    ws = col(6, 9)
    wg = col(9, 11)
    qk_gain = jnp.concatenate([jnp.tile(q_norm, (1, SW_Q_HEADS)) * (SW_HEAD_DIM ** -0.5),
                               jnp.tile(k_norm, (1, SW_KV_HEADS))], axis=1).reshape(depth, 1, -1).astype(F32)
    gate_bias = jnp.concatenate([b_igate, b_fgate], axis=1).reshape(depth, -1, 1).astype(F32)
    sink = jnp.broadcast_to(sinks.astype(F32)[:, :, None], (depth, SW_Q_HEADS, V7X_LANES))
    ml_gain = ml_out_norm.astype(F32)
    f1g, f1wi, f1wo = rows(ffn1_norm), bf(ffn1_wi), bf(ffn1_wo)
    f2g, f2wi, f2wo = rows(ffn2_norm), bf(ffn2_wi), bf(ffn2_wo)
    mixg, pleg = rows(mix_norm), rows(ple_norm)
    wa, wb, wo, wpg, wpp = bf(w_a), bf(w_b), bf(w_out), bf(w_ple_gate), bf(w_ple)
    p3 = p.reshape(depth, n_tok, P_DIM)

    bias2 = _swa_bias(rel_bias)
    head_of = np.arange(SW_Q_W + SW_KV_W) // SW_HEAD_DIM
    group_ones = jnp.asarray(head_of[:, None] == head_of[None, :], BF16)

    flat = lambda a: a.reshape(n_tok, a.shape[-1])
    x2 = x.reshape(n_tok, D_MODEL)
    for i in range(depth):
        x2 = _ffn(x2, f1g, f1wi, f1wo, i, tm)
        mq, mkt, mv, mo, gl, gc, sq, skt, sv, ga, gb = _inproj(
            x2.reshape(B, S, D_MODEL), mixg, wm, wkt, wift, gate_bias, ws, wg, qk_gain, group_ones, i, tm)
        ha = _mlstm(mq, mkt, mv, mo, gl, gc, ml_gain, i)
        hb = _swa(sq, skt, sv, bias2, sink, i)
        x2 = _post(x2, flat(ha), flat(hb), flat(ga), flat(gb), p3, wa, wb, wo,
                   f2g, f2wi, f2wo, pleg, wpg, wpp, i, tm)
    return x2.reshape(B, S, D_MODEL)
```

```python
import functools
import math

import jax
import jax.numpy as jnp
import numpy as np
from jax.experimental import pallas as pl
from jax.experimental.pallas import tpu as pltpu

D_MODEL = 1024
P_DIM = 256
D_FF = 2816
ML_HEADS = 4
ML_QK_DIM = 64
ML_V_DIM = 128
SW_Q_HEADS = 8
SW_KV_HEADS = 2
SW_HEAD_DIM = 64
WINDOW = 128
REL_BUCKETS = 32
REL_MAX_DIST = 128
EPS = 1e-6

ML_QK_W = ML_HEADS * ML_QK_DIM
ML_V_W = ML_HEADS * ML_V_DIM
SW_Q_W = SW_Q_HEADS * SW_HEAD_DIM
SW_KV_W = SW_KV_HEADS * SW_HEAD_DIM
SW_GROUP = SW_Q_HEADS // SW_KV_HEADS
SPLITS = (ML_QK_W, ML_QK_W, ML_V_W, ML_V_W, ML_HEADS, ML_HEADS, SW_Q_W, SW_KV_W, SW_KV_W, D_MODEL, D_MODEL)

V7X_LANES = 128
V7X_VMEM_LIMIT_BYTES = 56 * 1024 * 1024

FFN_CHUNK = 256
ML_CHUNK = 128
TOKEN_TILE = 512

F32 = jnp.float32
BF16 = jnp.bfloat16


def _dot(a, b):
    return jnp.dot(a, b, preferred_element_type=F32)


def lax_rsqrt(v):
    return jax.lax.rsqrt(v)


def _rms_scale(xf):
    return lax_rsqrt(jnp.mean(xf * xf, axis=-1, keepdims=True) + EPS)


def _sigmoid(v):
    return 1.0 / (1.0 + jnp.exp(-v))


def _const_spec(shape):
    nd = len(shape)
    return pl.BlockSpec(shape, lambda *_: (0,) * nd, pipeline_mode=pl.Buffered(1))


def _layer_spec(stacked, layer):
    tail = stacked.shape[1:]
    return pl.BlockSpec((None,) + tail, lambda *_: (layer,) + (0,) * len(tail), pipeline_mode=pl.Buffered(1))


def _params(*sem, flags=None):
    return pltpu.CompilerParams(dimension_semantics=sem, vmem_limit_bytes=V7X_VMEM_LIMIT_BYTES, flags=flags)


def _ffn_half_step(xf, g_ref, wi_ref, wo_ref, n_ref, act_ref):
    n_ref[...] = (xf * _rms_scale(xf) * g_ref[...]).astype(BF16)
    for j in range(D_FF // FFN_CHUNK):
        lo = j * FFN_CHUNK
        hg = _dot(n_ref[...], wi_ref[:, lo:lo + FFN_CHUNK])
        hu = _dot(n_ref[...], wi_ref[:, D_FF + lo:D_FF + lo + FFN_CHUNK])
        act_ref[:, lo:lo + FFN_CHUNK] = (hg * _sigmoid(hg) * hu).astype(BF16)
    return xf + 0.5 * _dot(act_ref[...], wo_ref[...])


def _ffn_body(x_ref, g_ref, wi_ref, wo_ref, o_ref, n_ref, act_ref):
    o_ref[...] = _ffn_half_step(x_ref[...], g_ref, wi_ref, wo_ref, n_ref, act_ref)


def _ffn(x2, g, wi, wo, layer, tm):
    n_tok = x2.shape[0]
    return pl.pallas_call(
        _ffn_body,
        grid=(n_tok // tm,),
        in_specs=[pl.BlockSpec((tm, D_MODEL), lambda i: (i, 0)),
                  _layer_spec(g, layer), _layer_spec(wi, layer), _layer_spec(wo, layer)],
        out_specs=pl.BlockSpec((tm, D_MODEL), lambda i: (i, 0)),
        out_shape=jax.ShapeDtypeStruct((n_tok, D_MODEL), F32),
        scratch_shapes=[pltpu.VMEM((tm, D_MODEL), BF16), pltpu.VMEM((tm, D_FF), BF16)],
        compiler_params=_params("parallel"),
        name="ffn",
    )(x2, g, wi, wo)


def _inproj_body(x_ref, g_ref, wm_ref, wkt_ref, wift_ref, bif_ref, ws_ref, wg_ref, gain_ref, ones_ref,
                 mq_ref, mkt_ref, mv_ref, mo_ref, gl_ref, gc_ref, sq_ref, skt_ref, sv_ref, ga_ref, gb_ref):
    xf = x_ref[...]
    u = (xf * _rms_scale(xf) * g_ref[...]).astype(BF16)
    zm = _dot(u, wm_ref[...])
    mq_ref[...] = zm[:, :ML_QK_W].astype(BF16)
    mv_ref[...] = zm[:, ML_QK_W:ML_QK_W + ML_V_W].astype(BF16)
    mo_ref[...] = zm[:, ML_QK_W + ML_V_W:].astype(BF16)
    nt = (((1,), (1,)), ((), ()))
    mkt_ref[...] = (jax.lax.dot_general(wkt_ref[...], u, nt, preferred_element_type=F32)
                    * (1.0 / math.sqrt(ML_QK_DIM))).astype(BF16)
    pre = jax.lax.dot_general(wift_ref[...], u, nt, preferred_element_type=F32) + bif_ref[...]
    logf = jnp.minimum(pre, 0.0) - jnp.log1p(jnp.exp(-jnp.abs(pre)))
    pos = jax.lax.broadcasted_iota(jnp.int32, pre.shape, 1) & (ML_CHUNK - 1)
    csum = logf
    k = 1
    while k < ML_CHUNK:
        csum = csum + jnp.where(pos >= k, pltpu.roll(csum, k, 1), 0.0)
        k *= 2
    gl_ref[...] = jnp.where(jax.lax.broadcasted_iota(jnp.int32, pre.shape, 0) < ML_HEADS, pre, logf)
    gc_ref[...] = csum
    zs = _dot(u, ws_ref[...])
    qk = zs[:, :SW_Q_W + SW_KV_W]
    ssq = _dot((qk * qk).astype(BF16), ones_ref[...])
    qkn = qk * lax_rsqrt(ssq * (1.0 / SW_HEAD_DIM) + EPS) * gain_ref[...]
    sq_ref[...] = qkn[:, :SW_Q_W].astype(BF16)
    skt_ref[...] = qkn[:, SW_Q_W:].T.astype(BF16)
    sv_ref[...] = zs[:, SW_Q_W + SW_KV_W:].astype(BF16)
    zg = _dot(u, wg_ref[...])
    ga_ref[...] = zg[:, :D_MODEL].astype(BF16)
    gb_ref[...] = zg[:, D_MODEL:].astype(BF16)


def _inproj(x3, g, wm, wkt, wift, bif, ws, wg, gain, ones, layer, tm):
    b, s, _ = x3.shape
    assert tm % ML_CHUNK == 0
    tok = lambda w: pl.BlockSpec((None, tm, w), lambda i, j: (i, j, 0))
    tok_t = lambda w: pl.BlockSpec((None, w, tm), lambda i, j: (i, 0, j))
    sd = jax.ShapeDtypeStruct
    return pl.pallas_call(
        _inproj_body,
        grid=(b, s // tm),
        in_specs=[tok(D_MODEL)] + [_layer_spec(a, layer) for a in (g, wm, wkt, wift, bif, ws, wg, gain)]
                 + [_const_spec(ones.shape)],
        out_specs=[tok(ML_QK_W), tok_t(ML_QK_W), tok(ML_V_W), tok(ML_V_W), tok_t(2 * ML_HEADS), tok_t(2 * ML_HEADS),
                   tok(SW_Q_W), tok_t(SW_KV_W), tok(SW_KV_W), tok(D_MODEL), tok(D_MODEL)],
        out_shape=[sd((b, s, ML_QK_W), BF16), sd((b, ML_QK_W, s), BF16), sd((b, s, ML_V_W), BF16),
                   sd((b, s, ML_V_W), BF16), sd((b, 2 * ML_HEADS, s), F32), sd((b, 2 * ML_HEADS, s), F32),
                   sd((b, s, SW_Q_W), BF16), sd((b, SW_KV_W, s), BF16), sd((b, s, SW_KV_W), BF16),
                   sd((b, s, D_MODEL), BF16), sd((b, s, D_MODEL), BF16)],
        compiler_params=_params("parallel", "parallel"),
        name="inproj",
    )(x3, g, wm, wkt, wift, bif, ws, wg, gain, ones)


def _mlstm_pieces(q_ref, kt_ref, v_ref, o_ref, gl_ref, gc_ref, gain_ref, h_ref, c_ref, m_ref):
    L = ML_CHUNK
    H, dk, dv = ML_HEADS, ML_QK_DIM, ML_V_DIM
    heads = range(H)
    t_idx = jax.lax.broadcasted_iota(jnp.int32, (L, H * L), 0)
    s_idx = jax.lax.broadcasted_iota(jnp.int32, (L, H * L), 1)
    causal = (s_idx & (L - 1)) <= t_idx
    ones_col = (jax.lax.broadcasted_iota(jnp.int32, (L, V7X_LANES), 1) == 0).astype(BF16)
    r_blk = jax.lax.broadcasted_iota(jnp.int32, (H * dk, H * L), 0) // dk
    c_blk = jax.lax.broadcasted_iota(jnp.int32, (H * dk, H * L), 1) // L
    zeros_c = jnp.zeros((dk, 2 * dv), BF16)
    zeros_k = jnp.zeros((dk, 2 * dk), BF16)

    def chunk(c):
        sl = slice(c * L, (c + 1) * L)
        st = {}

        def gates():
            gl = gl_ref[:, sl]
            gc = gc_ref[:, sl]
            b_rows = [gc[H + h:H + h + 1, :] for h in heads]
            c_rows = [gl[h:h + 1, :] - b_rows[h] for h in heads]
            st["m_prev"] = [m_ref[h:h + 1, 0:1] for h in heads]
            st["g_tot"] = [b_rows[h][:, L - 1:L] for h in heads]
            st["a_row"] = [st["g_tot"][h] + c_rows[h] for h in heads]
            st["m_new"] = [jnp.maximum(st["g_tot"][h] + st["m_prev"][h],
                                       jnp.max(st["a_row"][h], axis=-1, keepdims=True)) for h in heads]
            st["dm"] = jnp.where(causal, jnp.concatenate(c_rows, axis=1), -jnp.inf)
            st["bm"] = jnp.where(causal, jnp.concatenate([gl[H + h:H + h + 1, :] for h in heads], axis=1), 0.0)

        def scores():
            k_bd = jnp.where(r_blk == c_blk, jnp.concatenate([kt_ref[:, sl]] * H, axis=1), jnp.zeros((), BF16))
            st["s_all"] = _dot(q_ref[sl, :], k_bd)

        def weights():
            dm, bm = st["dm"], st["bm"]
            st["m_col"] = [jnp.maximum(jnp.max(dm[:, h * L:(h + 1) * L], axis=-1, keepdims=True), st["m_prev"][h])
                           for h in heads]
            st["b_col"] = [jnp.sum(bm[:, h * L:(h + 1) * L], axis=-1, keepdims=True) for h in heads]
            m_all = jnp.concatenate([jnp.broadcast_to(st["m_col"][h], (L, L)) for h in heads], axis=1)
            st["w_all"] = (jnp.exp(dm - m_all) * st["s_all"]).astype(BF16)
            st["res"] = [None] * H

        def head_matmul(h):
            def run():
                pair = h // 2
                v_aug = jnp.concatenate([v_ref[sl, h * dv:(h + 1) * dv], ones_col], axis=1)
                st.setdefault("c_aug", {})[h] = c_ref[h]
                c_bf = st["c_aug"][h].astype(BF16)
                rhs = jnp.concatenate([v_aug] + ([c_bf, zeros_c] if h % 2 == 0 else [zeros_c, c_bf]), axis=0)
                s_inter = jnp.exp(st["m_prev"][h] - st["m_col"][h])
                q_s = (q_ref[sl, pair * 2 * dk:(pair + 1) * 2 * dk].astype(F32) * s_inter).astype(BF16)
                kw = (kt_ref[h * dk:(h + 1) * dk, sl].astype(F32)
                      * jnp.exp(st["a_row"][h] - st["m_new"][h])).astype(BF16)
                lhs = jnp.concatenate([jnp.concatenate([st["w_all"][:, h * L:(h + 1) * L], q_s], axis=1),
                                       jnp.concatenate([kw, zeros_k], axis=1)], axis=0)
                st["res"][h] = _dot(lhs, rhs)
            return run

        def update():
            for h in heads:
                decay = jnp.exp(st["g_tot"][h] + st["m_prev"][h] - st["m_new"][h])
                c_ref[h] = decay * st["c_aug"][h] + st["res"][h][L:, :]
                m_ref[h:h + 1, :] = jnp.broadcast_to(st["m_new"][h], (1, V7X_LANES))

        def output(h):
            def run():
                res = st["res"][h]
                inv = 1.0 / jnp.maximum(jnp.abs(res[:L, dv:dv + 1]), jnp.exp(-(st["b_col"][h] + st["m_col"][h])))
                hh = res[:L, :dv] * inv
                hn = hh * lax_rsqrt(jnp.mean(hh * hh, axis=-1, keepdims=True) + EPS) * gain_ref[h:h + 1, :]
                gate = _sigmoid(o_ref[sl, h * dv:(h + 1) * dv].astype(F32))
                h_ref[sl, h * dv:(h + 1) * dv] = (gate * hn).astype(BF16)
            return run

        return ([gates, scores, weights] + [head_matmul(h) for h in heads] + [update]
                + [output(h) for h in heads])

    pieces = []
    for c in range(q_ref.shape[0] // L):
        pieces += chunk(c)
    return pieces


def _swa_pieces(q_ref, kt_ref, ktp_ref, v_ref, vp_ref, bias_ref, first_sel, sink_ref, o_ref):
    W, d = WINDOW, SW_HEAD_DIM
    pairs = SW_GROUP // 2
    rows = pairs * W
    groups = range(SW_KV_HEADS)
    lane = jax.lax.broadcasted_iota(jnp.int32, (rows, 2 * d), 1)
    zeros_kt = jnp.zeros((d, 2 * W), BF16)
    ones_v = jnp.ones((2 * W, 2 * d), BF16)

    def block(blk):
        cur = slice(blk * W, (blk + 1) * W)
        prev = slice((blk - 1) * W, blk * W)
        st = {}

        def scores():
            st["scores"], st["vv"] = [], []
            for g in groups:
                hd = slice(g * d, (g + 1) * d)
                kt_prev = ktp_ref[hd, :] if blk == 0 else kt_ref[hd, prev]
                v_prev = vp_ref[:, hd] if blk == 0 else v_ref[prev, hd]
                kt = jnp.concatenate([kt_prev, kt_ref[hd, cur]], axis=1)
                vg = jnp.concatenate([v_prev, v_ref[cur, hd]], axis=0)
                st["vv"].append(jnp.concatenate([vg, vg, ones_v], axis=1))
                kt_sel = jnp.concatenate([jnp.concatenate([kt, zeros_kt], axis=0),
                                          jnp.concatenate([zeros_kt, kt], axis=0)], axis=1)
                q_rows = jnp.concatenate([q_ref[cur, (g * pairs + pr) * 2 * d:(g * pairs + pr + 1) * 2 * d]
                                          for pr in range(pairs)], axis=0)
                st["scores"].append(_dot(q_rows, kt_sel))

        def softmax():
            st["probs"], st["tail"] = [], []
            for g in groups:
                for sub in range(2):
                    bias = bias_ref[first_sel, g, sub] if blk == 0 else bias_ref[1, g, sub]
                    logits = st["scores"][g][:, sub * 2 * W:(sub + 1) * 2 * W] + bias
                    sink = jnp.concatenate(
                        [jnp.broadcast_to(sink_ref[g * SW_GROUP + 2 * pr + sub:g * SW_GROUP + 2 * pr + sub + 1, :],
                                          (W, V7X_LANES)) for pr in range(pairs)], axis=0)
                    m = jnp.maximum(jnp.broadcast_to(jnp.max(logits, axis=-1, keepdims=True), (rows, V7X_LANES)),
                                    sink)
                    st["probs"].append(jnp.exp(logits - jnp.concatenate([m, m], axis=1)).astype(BF16))
                    st["tail"].append(jnp.exp(sink - m))

        def values():
            res = [_dot(st["probs"][2 * g + sub], st["vv"][g]) for g in groups for sub in range(2)]
            for g in groups:
                outs = [res[2 * g + sub][:, :2 * d] / (res[2 * g + sub][:, 2 * d:] + st["tail"][2 * g + sub])
                        for sub in range(2)]
                sel = jnp.where(lane < d, outs[0], outs[1]).astype(BF16)
                for pr in range(pairs):
                    o_ref[cur, (g * pairs + pr) * 2 * d:(g * pairs + pr + 1) * 2 * d] = sel[pr * W:(pr + 1) * W]

        return [scores, softmax, values]

    pieces = []
    for blk in range(q_ref.shape[0] // W):
        pieces += block(blk)
    return pieces


def _merge_lists(a, b):
    out, ia, ib = [], 0, 0
    while ia < len(a) or ib < len(b):
        if ib >= len(b) or (ia < len(a) and ia * len(b) <= ib * len(a)):
            out.append(a[ia]); ia += 1
        else:
            out.append(b[ib]); ib += 1
    return out


def _mixpost_body(x_ref, ga_ref, gb_ref, p_ref,
                  mq_ref, mkt_ref, mv_ref, mo_ref, gl_ref, gc_ref, sq_ref, skt_ref, sktp_ref, sv_ref, svp_ref,
                  bias_ref, sink_ref, mlg_ref, wa_ref, wb_ref, wo_ref, fg_ref, fwi_ref, fwo_ref,
                  pg_ref, pwg_ref, pwp_ref, o_ref,
                  n_ref, act_ref, ha_ref, hb_ref, c_ref, m_ref, *, tiles_per_seq):
    t = pl.program_id(0)
    n_tiles = pl.num_programs(0) - 1
    seq_start = jax.lax.rem(jnp.minimum(t, n_tiles - 1), tiles_per_seq) == 0

    @pl.when(seq_start)
    def _():
        c_ref[...] = jnp.zeros_like(c_ref)
        m_ref[...] = jnp.zeros_like(m_ref)

    first_sel = jnp.where(seq_start, 0, 1)

    def mixer_pieces():
        return _merge_lists(
            _mlstm_pieces(mq_ref, mkt_ref, mv_ref, mo_ref, gl_ref, gc_ref, mlg_ref, ha_ref, c_ref, m_ref),
            _swa_pieces(sq_ref, skt_ref, sktp_ref, sv_ref, svp_ref, bias_ref, first_sel, sink_ref, hb_ref))

    @pl.when(t == 0)
    def _():
        for piece in mixer_pieces():
            piece()

    @pl.when(t > 0)
    def _():
        ya = _dot(ha_ref[...], wa_ref[...])
        yb = _dot(hb_ref[...], wb_ref[...])
        mixed = _sigmoid(ga_ref[...].astype(F32)) * ya + _sigmoid(gb_ref[...].astype(F32)) * yb
        x1 = x_ref[...] + _dot(mixed.astype(BF16), wo_ref[...])
        n_ref[...] = (x1 * _rms_scale(x1) * fg_ref[...]).astype(BF16)
        pieces = mixer_pieces()
        n_chunks = D_FF // FFN_CHUNK
        for j in range(n_chunks):
            lo = j * FFN_CHUNK
            hg = _dot(n_ref[...], fwi_ref[:, lo:lo + FFN_CHUNK])
            hu = _dot(n_ref[...], fwi_ref[:, D_FF + lo:D_FF + lo + FFN_CHUNK])
            act_ref[:, lo:lo + FFN_CHUNK] = (hg * _sigmoid(hg) * hu).astype(BF16)
            for piece in pieces[j * len(pieces) // n_chunks:(j + 1) * len(pieces) // n_chunks]:
                piece()
        x2 = x1 + 0.5 * _dot(act_ref[...], fwo_ref[...])
        n_ref[...] = (x2 * _rms_scale(x2) * pg_ref[...]).astype(BF16)
        gate = _sigmoid(_dot(n_ref[...], pwg_ref[...]))
        o_ref[...] = x2 + gate * _dot(p_ref[...].astype(BF16), pwp_ref[...])


def _mixpost(x2, ga, gb, p3, mq, mkt, mv, mo, gl, gc, sq, skt, sv, bias2, sink, ml_gain,
             wa, wb, wo, fg, fwi, fwo, pg, pwg, pwp, layer, seq_len, tm):
    n_tok = x2.shape[0]
    n_tiles = n_tok // tm
    tps = seq_len // tm
    wpt = tm // WINDOW
    assert tm % WINDOW == 0 and tm % ML_CHUNK == 0 and seq_len % tm == 0
    mix = lambda t: jnp.minimum(t, n_tiles - 1)
    post = lambda t: jnp.maximum(t - 1, 0)
    tok = lambda w: pl.BlockSpec((tm, w), lambda t: (post(t), 0))
    seq = lambda w: pl.BlockSpec((None, tm, w), lambda t: (mix(t) // tps, mix(t) % tps, 0))
    seq_t = lambda w: pl.BlockSpec((None, w, tm), lambda t: (mix(t) // tps, 0, mix(t) % tps))
    before = lambda t: jnp.maximum((mix(t) % tps) * wpt - 1, 0)
    weights = (wa, wb, wo, fg, fwi, fwo, pg, pwg, pwp)
    return pl.pallas_call(
        functools.partial(_mixpost_body, tiles_per_seq=tps),
        grid=(n_tiles + 1,),
        in_specs=[tok(D_MODEL), tok(D_MODEL), tok(D_MODEL),
                  pl.BlockSpec((None, tm, P_DIM), lambda t: (layer, post(t), 0)),
                  seq(ML_QK_W), seq_t(ML_QK_W), seq(ML_V_W), seq(ML_V_W), seq_t(2 * ML_HEADS), seq_t(2 * ML_HEADS),
                  seq(SW_Q_W), seq_t(SW_KV_W),
                  pl.BlockSpec((None, SW_KV_W, WINDOW), lambda t: (mix(t) // tps, 0, before(t))),
                  seq(SW_KV_W),
                  pl.BlockSpec((None, WINDOW, SW_KV_W), lambda t: (mix(t) // tps, before(t), 0)),
                  _const_spec(bias2.shape), _layer_spec(sink, layer), _layer_spec(ml_gain, layer)]
                 + [_layer_spec(w, layer) for w in weights],
        out_specs=tok(D_MODEL),
        out_shape=jax.ShapeDtypeStruct((n_tok, D_MODEL), F32),
        scratch_shapes=[pltpu.VMEM((tm, D_MODEL), BF16), pltpu.VMEM((tm, D_FF), BF16),
                        pltpu.VMEM((tm, ML_V_W), BF16), pltpu.VMEM((tm, SW_Q_W), BF16),
                        pltpu.VMEM((ML_HEADS, ML_QK_DIM, 2 * ML_V_DIM), F32),
                        pltpu.VMEM((2 * ML_HEADS, V7X_LANES), F32)],
        compiler_params=_params("arbitrary"),
        name="mixpost",
    )(x2, ga, gb, p3, mq, mkt, mv, mo, gl, gc, sq, skt, skt, sv, sv, bias2, sink, ml_gain, *weights)


def _t5_bucket(dist):
    max_exact = REL_BUCKETS // 2
    d = np.maximum(dist, 0)
    large = max_exact + (np.log(np.maximum(d, 1) / max_exact) / np.log(REL_MAX_DIST / max_exact)
                         * (REL_BUCKETS - max_exact)).astype(np.int32)
    large = np.minimum(large, REL_BUCKETS - 1)
    return np.where(d < max_exact, d, large).astype(np.int32)


def _swa_bias(rel_bias):
    W = WINDOW
    pairs = SW_GROUP // 2
    dist = np.arange(W)[:, None] + W - np.arange(2 * W)[None, :]
    in_window = (dist >= 0) & (dist < W)
    onehot = (_t5_bucket(dist).reshape(-1)[None, :] == np.arange(REL_BUCKETS)[:, None]).astype(np.float32)
    bias = jnp.dot(rel_bias.astype(F32).T, onehot, precision=jax.lax.Precision.HIGHEST)
    bias = bias.reshape(SW_KV_HEADS, pairs, 2, W, 2 * W)
    bias = jnp.transpose(bias, (0, 2, 1, 3, 4)).reshape(SW_KV_HEADS, 2, pairs * W, 2 * W)
    mask_later = np.tile(in_window, (pairs, 1))
    mask_first = np.tile(in_window & (np.arange(2 * W) >= W)[None, :], (pairs, 1))
    return jnp.stack([jnp.where(mask_first, bias, -jnp.inf), jnp.where(mask_later, bias, -jnp.inf)])


def kernel(x, p, ffn1_norm, ffn1_wi, ffn1_wo, mix_norm, w_in, b_igate, b_fgate, ml_out_norm, q_norm, k_norm, sinks, rel_bias, w_a, w_b, w_out, ffn2_norm, ffn2_wi, ffn2_wo, ple_norm, w_ple_gate, w_ple):
    B, S, _ = x.shape
    depth = p.shape[0]
    n_tok = B * S
    tm = min(TOKEN_TILE, S)
    offs = [0] + [int(o) for o in np.cumsum(SPLITS)]

    bf = lambda w: w.astype(BF16)
    rows = lambda v: v.reshape(depth, 1, -1).astype(F32)
    w_in_b = bf(w_in)
    col = lambda a, b: w_in_b[:, :, offs[a]:offs[b]]
    wm = jnp.concatenate([col(0, 1), col(2, 4)], axis=2)
    wkt = jnp.swapaxes(col(1, 2), 1, 2)
    wift = jnp.swapaxes(col(4, 6), 1, 2)
    ws = col(6, 9)
    wg = col(9, 11)
    qk_gain = jnp.concatenate([jnp.tile(q_norm, (1, SW_Q_HEADS)) * (SW_HEAD_DIM ** -0.5),
                               jnp.tile(k_norm, (1, SW_KV_HEADS))], axis=1).reshape(depth, 1, -1).astype(F32)
    gate_bias = jnp.concatenate([b_igate, b_fgate], axis=1).reshape(depth, -1, 1).astype(F32)
    sink = jnp.broadcast_to(sinks.astype(F32)[:, :, None], (depth, SW_Q_HEADS, V7X_LANES))
    ml_gain = ml_out_norm.astype(F32)
    f1g, f1wi, f1wo = rows(ffn1_norm), bf(ffn1_wi), bf(ffn1_wo)
    f2g, f2wi, f2wo = rows(ffn2_norm), bf(ffn2_wi), bf(ffn2_wo)
    mixg, pleg = rows(mix_norm), rows(ple_norm)
    wa, wb, wo, wpg, wpp = bf(w_a), bf(w_b), bf(w_out), bf(w_ple_gate), bf(w_ple)
    p3 = p.reshape(depth, n_tok, P_DIM)

    bias2 = _swa_bias(rel_bias)
    head_of = np.arange(SW_Q_W + SW_KV_W) // SW_HEAD_DIM
    group_ones = jnp.asarray(head_of[:, None] == head_of[None, :], BF16)

    flat = lambda a: a.reshape(n_tok, a.shape[-1])
    x2 = x.reshape(n_tok, D_MODEL)
    for i in range(depth):
        x2 = _ffn(x2, f1g, f1wi, f1wo, i, tm)
        mq, mkt, mv, mo, gl, gc, sq, skt, sv, ga, gb = _inproj(
            x2.reshape(B, S, D_MODEL), mixg, wm, wkt, wift, gate_bias, ws, wg, qk_gain, group_ones, i, tm)
        x2 = _mixpost(x2, flat(ga), flat(gb), p3, mq, mkt, mv, mo, gl, gc, sq, skt, sv, bias2, sink, ml_gain,
                      wa, wb, wo, f2g, f2wi, f2wo, pleg, wpg, wpp, i, S, tm)
    return x2.reshape(B, S, D_MODEL)
```

```python
import functools
import math

import jax
import jax.numpy as jnp
import numpy as np
from jax.experimental import pallas as pl
from jax.experimental.pallas import tpu as pltpu

D_MODEL = 1024
P_DIM = 256
D_FF = 2816
ML_HEADS = 4
ML_QK_DIM = 64
ML_V_DIM = 128
SW_Q_HEADS = 8
SW_KV_HEADS = 2
SW_HEAD_DIM = 64
WINDOW = 128
REL_BUCKETS = 32
REL_MAX_DIST = 128
EPS = 1e-6

ML_QK_W = ML_HEADS * ML_QK_DIM
ML_V_W = ML_HEADS * ML_V_DIM
SW_Q_W = SW_Q_HEADS * SW_HEAD_DIM
SW_KV_W = SW_KV_HEADS * SW_HEAD_DIM
SW_GROUP = SW_Q_HEADS // SW_KV_HEADS
SPLITS = (ML_QK_W, ML_QK_W, ML_V_W, ML_V_W, ML_HEADS, ML_HEADS, SW_Q_W, SW_KV_W, SW_KV_W, D_MODEL, D_MODEL)

TOK_MV = ML_QK_W
TOK_MO = TOK_MV + ML_V_W
TOK_SQ = TOK_MO + ML_V_W
TOK_SV = TOK_SQ + SW_Q_W
TOK_W = TOK_SV + SW_KV_W
TR_W = ML_QK_W + SW_KV_W

V7X_LANES = 128
V7X_VMEM_LIMIT_BYTES = 56 * 1024 * 1024

FFN_CHUNK = 256
ML_CHUNK = 128
TOKEN_TILE = 512

F32 = jnp.float32
BF16 = jnp.bfloat16


def _dot(a, b):
    return jnp.dot(a, b, preferred_element_type=F32)


def lax_rsqrt(v):
    return jax.lax.rsqrt(v)


def _rms_scale(xf):
    return lax_rsqrt(jnp.mean(xf * xf, axis=-1, keepdims=True) + EPS)


def _sigmoid(v):
    return 1.0 / (1.0 + jnp.exp(-v))


def _const_spec(shape):
    nd = len(shape)
    return pl.BlockSpec(shape, lambda *_: (0,) * nd, pipeline_mode=pl.Buffered(1))


def _layer_spec(stacked, layer):
    tail = stacked.shape[1:]
    return pl.BlockSpec((None,) + tail, lambda *_: (layer,) + (0,) * len(tail), pipeline_mode=pl.Buffered(1))


def _params(*sem, flags=None):
    return pltpu.CompilerParams(dimension_semantics=sem, vmem_limit_bytes=V7X_VMEM_LIMIT_BYTES, flags=flags)


def _ffn_half_step(xf, g_ref, wi_ref, wo_ref, n_ref, act_ref):
    n_ref[...] = (xf * _rms_scale(xf) * g_ref[...]).astype(BF16)
    for j in range(D_FF // FFN_CHUNK):
        lo = j * FFN_CHUNK
        hg = _dot(n_ref[...], wi_ref[:, lo:lo + FFN_CHUNK])
        hu = _dot(n_ref[...], wi_ref[:, D_FF + lo:D_FF + lo + FFN_CHUNK])
        act_ref[:, lo:lo + FFN_CHUNK] = (hg * _sigmoid(hg) * hu).astype(BF16)
    return xf + 0.5 * _dot(act_ref[...], wo_ref[...])


def _ffn_body(x_ref, g_ref, wi_ref, wo_ref, o_ref, n_ref, act_ref):
    o_ref[...] = _ffn_half_step(x_ref[...], g_ref, wi_ref, wo_ref, n_ref, act_ref)


def _ffn(x2, g, wi, wo, layer, tm):
    n_tok = x2.shape[0]
    return pl.pallas_call(
        _ffn_body,
        grid=(n_tok // tm,),
        in_specs=[pl.BlockSpec((tm, D_MODEL), lambda i: (i, 0)),
                  _layer_spec(g, layer), _layer_spec(wi, layer), _layer_spec(wo, layer)],
        out_specs=pl.BlockSpec((tm, D_MODEL), lambda i: (i, 0)),
        out_shape=jax.ShapeDtypeStruct((n_tok, D_MODEL), F32),
        scratch_shapes=[pltpu.VMEM((tm, D_MODEL), BF16), pltpu.VMEM((tm, D_FF), BF16)],
        compiler_params=_params("parallel"),
        name="ffn",
    )(x2, g, wi, wo)


def _inproj_body(x_ref, g_ref, wm_ref, wkt_ref, wift_ref, bif_ref, ws_ref, wg_ref, gain_ref, ones_ref,
                 tok_ref, gate_ref, tr_ref, gt_ref):
    xf = x_ref[...]
    u = (xf * _rms_scale(xf) * g_ref[...]).astype(BF16)
    tok_ref[:, :TOK_SQ] = _dot(u, wm_ref[...]).astype(BF16)
    nt = (((1,), (1,)), ((), ()))
    tr_ref[:ML_QK_W, :] = (jax.lax.dot_general(wkt_ref[...], u, nt, preferred_element_type=F32)
                           * (1.0 / math.sqrt(ML_QK_DIM))).astype(BF16)
    pre = jax.lax.dot_general(wift_ref[...], u, nt, preferred_element_type=F32) + bif_ref[...]
    logf = jnp.minimum(pre, 0.0) - jnp.log1p(jnp.exp(-jnp.abs(pre)))
    pos = jax.lax.broadcasted_iota(jnp.int32, pre.shape, 1) & (ML_CHUNK - 1)
    csum = logf
    k = 1
    while k < ML_CHUNK:
        csum = csum + jnp.where(pos >= k, pltpu.roll(csum, k, 1), 0.0)
        k *= 2
    gt_ref[:2 * ML_HEADS, :] = jnp.where(jax.lax.broadcasted_iota(jnp.int32, pre.shape, 0) < ML_HEADS, pre, logf)
    gt_ref[2 * ML_HEADS:, :] = csum
    zs = _dot(u, ws_ref[...])
    qk = zs[:, :SW_Q_W + SW_KV_W]
    ssq = _dot((qk * qk).astype(BF16), ones_ref[...])
    qkn = qk * lax_rsqrt(ssq * (1.0 / SW_HEAD_DIM) + EPS) * gain_ref[...]
    tok_ref[:, TOK_SQ:TOK_SV] = qkn[:, :SW_Q_W].astype(BF16)
    tr_ref[ML_QK_W:, :] = qkn[:, SW_Q_W:].T.astype(BF16)
    tok_ref[:, TOK_SV:] = zs[:, SW_Q_W + SW_KV_W:].astype(BF16)
    gate_ref[...] = _dot(u, wg_ref[...]).astype(BF16)


def _inproj(x3, g, wm, wkt, wift, bif, ws, wg, gain, ones, layer, tm):
    b, s, _ = x3.shape
    assert tm % ML_CHUNK == 0
    tok = lambda w: pl.BlockSpec((None, tm, w), lambda i, j: (i, j, 0))
    tok_t = lambda w: pl.BlockSpec((None, w, tm), lambda i, j: (i, 0, j))
    sd = jax.ShapeDtypeStruct
    return pl.pallas_call(
        _inproj_body,
        grid=(b, s // tm),
        in_specs=[tok(D_MODEL)] + [_layer_spec(a, layer) for a in (g, wm, wkt, wift, bif, ws, wg, gain)]
                 + [_const_spec(ones.shape)],
        out_specs=[tok(TOK_W), tok(2 * D_MODEL), tok_t(TR_W), tok_t(4 * ML_HEADS)],
        out_shape=[sd((b, s, TOK_W), BF16), sd((b, s, 2 * D_MODEL), BF16), sd((b, TR_W, s), BF16),
                   sd((b, 4 * ML_HEADS, s), F32)],
        compiler_params=_params("parallel", "parallel"),
        name="inproj",
    )(x3, g, wm, wkt, wift, bif, ws, wg, gain, ones)


def _mlstm_pieces(q_ref, kt_ref, v_ref, o_ref, gl_ref, gc_ref, gain_ref, h_ref, c_ref, m_ref):
    L = ML_CHUNK
    H, dk, dv = ML_HEADS, ML_QK_DIM, ML_V_DIM
    heads = range(H)
    t_idx = jax.lax.broadcasted_iota(jnp.int32, (L, H * L), 0)
    s_idx = jax.lax.broadcasted_iota(jnp.int32, (L, H * L), 1)
    causal = (s_idx & (L - 1)) <= t_idx
    ones_col = (jax.lax.broadcasted_iota(jnp.int32, (L, V7X_LANES), 1) == 0).astype(BF16)
    r_blk = jax.lax.broadcasted_iota(jnp.int32, (H * dk, H * L), 0) // dk
    c_blk = jax.lax.broadcasted_iota(jnp.int32, (H * dk, H * L), 1) // L
    zeros_c = jnp.zeros((dk, 2 * dv), BF16)
    zeros_k = jnp.zeros((dk, 2 * dk), BF16)

    def chunk(c):
        sl = slice(c * L, (c + 1) * L)
        st = {}

        def gates():
            gl = gl_ref[:, sl]
            gc = gc_ref[:, sl]
            b_rows = [gc[H + h:H + h + 1, :] for h in heads]
            c_rows = [gl[h:h + 1, :] - b_rows[h] for h in heads]
            st["m_prev"] = [m_ref[h:h + 1, 0:1] for h in heads]
            st["g_tot"] = [b_rows[h][:, L - 1:L] for h in heads]
            st["a_row"] = [st["g_tot"][h] + c_rows[h] for h in heads]
            st["m_new"] = [jnp.maximum(st["g_tot"][h] + st["m_prev"][h],
                                       jnp.max(st["a_row"][h], axis=-1, keepdims=True)) for h in heads]
            st["dm"] = jnp.where(causal, jnp.concatenate(c_rows, axis=1), -jnp.inf)
            st["bm"] = jnp.where(causal, jnp.concatenate([gl[H + h:H + h + 1, :] for h in heads], axis=1), 0.0)

        def scores():
            k_bd = jnp.where(r_blk == c_blk, jnp.concatenate([kt_ref[:, sl]] * H, axis=1), jnp.zeros((), BF16))
            st["s_all"] = _dot(q_ref[sl, :], k_bd)

        def weights():
            dm, bm = st["dm"], st["bm"]
            st["m_col"] = [jnp.maximum(jnp.max(dm[:, h * L:(h + 1) * L], axis=-1, keepdims=True), st["m_prev"][h])
                           for h in heads]
            st["b_col"] = [jnp.sum(bm[:, h * L:(h + 1) * L], axis=-1, keepdims=True) for h in heads]
            m_all = jnp.concatenate([jnp.broadcast_to(st["m_col"][h], (L, L)) for h in heads], axis=1)
            st["w_all"] = (jnp.exp(dm - m_all) * st["s_all"]).astype(BF16)
            st["res"] = [None] * H

        def head_matmul(h):
            def run():
                pair = h // 2
                v_aug = jnp.concatenate([v_ref[sl, h * dv:(h + 1) * dv], ones_col], axis=1)
                st.setdefault("c_aug", {})[h] = c_ref[h]
                c_bf = st["c_aug"][h].astype(BF16)
                rhs = jnp.concatenate([v_aug] + ([c_bf, zeros_c] if h % 2 == 0 else [zeros_c, c_bf]), axis=0)
                s_inter = jnp.exp(st["m_prev"][h] - st["m_col"][h])
                q_s = (q_ref[sl, pair * 2 * dk:(pair + 1) * 2 * dk].astype(F32) * s_inter).astype(BF16)
                kw = (kt_ref[h * dk:(h + 1) * dk, sl].astype(F32)
                      * jnp.exp(st["a_row"][h] - st["m_new"][h])).astype(BF16)
                lhs = jnp.concatenate([jnp.concatenate([st["w_all"][:, h * L:(h + 1) * L], q_s], axis=1),
                                       jnp.concatenate([kw, zeros_k], axis=1)], axis=0)
                st["res"][h] = _dot(lhs, rhs)
            return run

        def update():
            for h in heads:
                decay = jnp.exp(st["g_tot"][h] + st["m_prev"][h] - st["m_new"][h])
                c_ref[h] = decay * st["c_aug"][h] + st["res"][h][L:, :]
                m_ref[h:h + 1, :] = jnp.broadcast_to(st["m_new"][h], (1, V7X_LANES))

        def output(h):
            def run():
                res = st["res"][h]
                inv = 1.0 / jnp.maximum(jnp.abs(res[:L, dv:dv + 1]), jnp.exp(-(st["b_col"][h] + st["m_col"][h])))
                hh = res[:L, :dv] * inv
                hn = hh * lax_rsqrt(jnp.mean(hh * hh, axis=-1, keepdims=True) + EPS) * gain_ref[h:h + 1, :]
                gate = _sigmoid(o_ref[sl, h * dv:(h + 1) * dv].astype(F32))
                h_ref[sl, h * dv:(h + 1) * dv] = (gate * hn).astype(BF16)
            return run

        return ([gates, scores, weights] + [head_matmul(h) for h in heads] + [update]
                + [output(h) for h in heads])

    pieces = []
    for c in range(q_ref.shape[0] // L):
        pieces += chunk(c)
    return pieces


def _swa_pieces(q_ref, kt_ref, ktp_ref, v_ref, vp_ref, bias_ref, first_sel, sink_ref, o_ref):
    W, d = WINDOW, SW_HEAD_DIM
    pairs = SW_GROUP // 2
    rows = pairs * W
    groups = range(SW_KV_HEADS)
    lane = jax.lax.broadcasted_iota(jnp.int32, (rows, 2 * d), 1)
    zeros_kt = jnp.zeros((d, 2 * W), BF16)
    ones_v = jnp.ones((2 * W, 2 * d), BF16)

    def block(blk):
        cur = slice(blk * W, (blk + 1) * W)
        prev = slice((blk - 1) * W, blk * W)
        st = {}

        def scores():
            st["scores"], st["vv"] = [], []
            for g in groups:
                hd = slice(g * d, (g + 1) * d)
                kt_prev = ktp_ref[hd, :] if blk == 0 else kt_ref[hd, prev]
                v_prev = vp_ref[:, hd] if blk == 0 else v_ref[prev, hd]
                kt = jnp.concatenate([kt_prev, kt_ref[hd, cur]], axis=1)
                vg = jnp.concatenate([v_prev, v_ref[cur, hd]], axis=0)
                st["vv"].append(jnp.concatenate([vg, vg, ones_v], axis=1))
                kt_sel = jnp.concatenate([jnp.concatenate([kt, zeros_kt], axis=0),
                                          jnp.concatenate([zeros_kt, kt], axis=0)], axis=1)
                q_rows = jnp.concatenate([q_ref[cur, (g * pairs + pr) * 2 * d:(g * pairs + pr + 1) * 2 * d]
                                          for pr in range(pairs)], axis=0)
                st["scores"].append(_dot(q_rows, kt_sel))

        def softmax():
            st["probs"], st["tail"] = [], []
            for g in groups:
                for sub in range(2):
                    bias = bias_ref[first_sel, g, sub] if blk == 0 else bias_ref[1, g, sub]
                    logits = st["scores"][g][:, sub * 2 * W:(sub + 1) * 2 * W] + bias
                    sink = jnp.concatenate(
                        [jnp.broadcast_to(sink_ref[g * SW_GROUP + 2 * pr + sub:g * SW_GROUP + 2 * pr + sub + 1, :],
                                          (W, V7X_LANES)) for pr in range(pairs)], axis=0)
                    m = jnp.maximum(jnp.broadcast_to(jnp.max(logits, axis=-1, keepdims=True), (rows, V7X_LANES)),
                                    sink)
                    st["probs"].append(jnp.exp(logits - jnp.concatenate([m, m], axis=1)).astype(BF16))
                    st["tail"].append(jnp.exp(sink - m))

        def values():
            res = [_dot(st["probs"][2 * g + sub], st["vv"][g]) for g in groups for sub in range(2)]
            for g in groups:
                outs = [res[2 * g + sub][:, :2 * d] / (res[2 * g + sub][:, 2 * d:] + st["tail"][2 * g + sub])
                        for sub in range(2)]
                sel = jnp.where(lane < d, outs[0], outs[1]).astype(BF16)
                for pr in range(pairs):
                    o_ref[cur, (g * pairs + pr) * 2 * d:(g * pairs + pr + 1) * 2 * d] = sel[pr * W:(pr + 1) * W]

        return [scores, softmax, values]

    pieces = []
    for blk in range(q_ref.shape[0] // W):
        pieces += block(blk)
    return pieces


def _merge_lists(a, b):
    out, ia, ib = [], 0, 0
    while ia < len(a) or ib < len(b):
        if ib >= len(b) or (ia < len(a) and ia * len(b) <= ib * len(a)):
            out.append(a[ia]); ia += 1
        else:
            out.append(b[ib]); ib += 1
    return out


def _mixpost_body(x_ref, gate_ref, p_ref, tok_ref, tr_ref, gt_ref,
                  bias_ref, sink_ref, mlg_ref, wa_ref, wb_ref, wo_ref, fg_ref, fwi_ref, fwo_ref,
                  pg_ref, pwg_ref, pwp_ref, o_ref,
                  n_ref, act_ref, ha_ref, hb_ref, c_ref, m_ref, ktp_ref, vp_ref, *, tiles_per_seq):
    t = pl.program_id(0)
    n_tiles = pl.num_programs(0) - 1
    seq_start = jax.lax.rem(jnp.minimum(t, n_tiles - 1), tiles_per_seq) == 0

    @pl.when(seq_start)
    def _():
        c_ref[...] = jnp.zeros_like(c_ref)
        m_ref[...] = jnp.zeros_like(m_ref)
        ktp_ref[...] = jnp.zeros_like(ktp_ref)
        vp_ref[...] = jnp.zeros_like(vp_ref)

    first_sel = jnp.where(seq_start, 0, 1)
    cols = lambda lo, hi: tok_ref.at[:, lo:hi]
    mq_ref, mv_ref, mo_ref = cols(0, TOK_MV), cols(TOK_MV, TOK_MO), cols(TOK_MO, TOK_SQ)
    sq_ref, sv_ref = cols(TOK_SQ, TOK_SV), cols(TOK_SV, TOK_W)
    mkt_ref, skt_ref = tr_ref.at[:ML_QK_W, :], tr_ref.at[ML_QK_W:, :]
    gl_ref, gc_ref = gt_ref.at[:2 * ML_HEADS, :], gt_ref.at[2 * ML_HEADS:, :]

    def keep_last_block():
        ktp_ref[...] = skt_ref[:, skt_ref.shape[1] - WINDOW:]
        vp_ref[...] = sv_ref[sv_ref.shape[0] - WINDOW:, :]

    def mixer_pieces():
        return _merge_lists(
            _mlstm_pieces(mq_ref, mkt_ref, mv_ref, mo_ref, gl_ref, gc_ref, mlg_ref, ha_ref, c_ref, m_ref),
            _swa_pieces(sq_ref, skt_ref, ktp_ref, sv_ref, vp_ref, bias_ref, first_sel, sink_ref, hb_ref)
        ) + [keep_last_block]

    @pl.when(t == 0)
    def _():
        for piece in mixer_pieces():
            piece()

    @pl.when(t > 0)
    def _():
        ya = _dot(ha_ref[...], wa_ref[...])
        yb = _dot(hb_ref[...], wb_ref[...])
        mixed = (_sigmoid(gate_ref[:, :D_MODEL].astype(F32)) * ya
                 + _sigmoid(gate_ref[:, D_MODEL:].astype(F32)) * yb)
        x1 = x_ref[...] + _dot(mixed.astype(BF16), wo_ref[...])
        n_ref[...] = (x1 * _rms_scale(x1) * fg_ref[...]).astype(BF16)
        pieces = mixer_pieces()
        n_chunks = D_FF // FFN_CHUNK
        for j in range(n_chunks):
            lo = j * FFN_CHUNK
            hg = _dot(n_ref[...], fwi_ref[:, lo:lo + FFN_CHUNK])
            hu = _dot(n_ref[...], fwi_ref[:, D_FF + lo:D_FF + lo + FFN_CHUNK])
            act_ref[:, lo:lo + FFN_CHUNK] = (hg * _sigmoid(hg) * hu).astype(BF16)
            for piece in pieces[j * len(pieces) // n_chunks:(j + 1) * len(pieces) // n_chunks]:
                piece()
        x2 = x1 + 0.5 * _dot(act_ref[...], fwo_ref[...])
        n_ref[...] = (x2 * _rms_scale(x2) * pg_ref[...]).astype(BF16)
        gate = _sigmoid(_dot(n_ref[...], pwg_ref[...]))
        o_ref[...] = x2 + gate * _dot(p_ref[...].astype(BF16), pwp_ref[...])


def _mixpost(x2, gate, p3, tok_major, transposed, gate_rows, bias2, sink, ml_gain,
             wa, wb, wo, fg, fwi, fwo, pg, pwg, pwp, layer, seq_len, tm):
    n_tok = x2.shape[0]
    n_tiles = n_tok // tm
    tps = seq_len // tm
    assert tm % WINDOW == 0 and tm % ML_CHUNK == 0 and seq_len % tm == 0
    mix = lambda t: jnp.minimum(t, n_tiles - 1)
    post = lambda t: jnp.maximum(t - 1, 0)
    tok = lambda w: pl.BlockSpec((tm, w), lambda t: (post(t), 0))
    seq = lambda w: pl.BlockSpec((None, tm, w), lambda t: (mix(t) // tps, mix(t) % tps, 0))
    seq_t = lambda w: pl.BlockSpec((None, w, tm), lambda t: (mix(t) // tps, 0, mix(t) % tps))
    weights = (wa, wb, wo, fg, fwi, fwo, pg, pwg, pwp)
    return pl.pallas_call(
        functools.partial(_mixpost_body, tiles_per_seq=tps),
        grid=(n_tiles + 1,),
        in_specs=[tok(D_MODEL), tok(2 * D_MODEL),
                  pl.BlockSpec((None, tm, P_DIM), lambda t: (layer, post(t), 0)),
                  seq(TOK_W), seq_t(TR_W), seq_t(4 * ML_HEADS),
                  _const_spec(bias2.shape), _layer_spec(sink, layer), _layer_spec(ml_gain, layer)]
                 + [_layer_spec(w, layer) for w in weights],
        out_specs=tok(D_MODEL),
        out_shape=jax.ShapeDtypeStruct((n_tok, D_MODEL), F32),
        scratch_shapes=[pltpu.VMEM((tm, D_MODEL), BF16), pltpu.VMEM((tm, D_FF), BF16),
                        pltpu.VMEM((tm, ML_V_W), BF16), pltpu.VMEM((tm, SW_Q_W), BF16),
                        pltpu.VMEM((ML_HEADS, ML_QK_DIM, 2 * ML_V_DIM), F32),
                        pltpu.VMEM((2 * ML_HEADS, V7X_LANES), F32),
                        pltpu.VMEM((SW_KV_W, WINDOW), BF16), pltpu.VMEM((WINDOW, SW_KV_W), BF16)],
        compiler_params=_params("arbitrary"),
        name="mixpost",
    )(x2, gate, p3, tok_major, transposed, gate_rows, bias2, sink, ml_gain, *weights)


def _t5_bucket(dist):
    max_exact = REL_BUCKETS // 2
    d = np.maximum(dist, 0)
    large = max_exact + (np.log(np.maximum(d, 1) / max_exact) / np.log(REL_MAX_DIST / max_exact)
                         * (REL_BUCKETS - max_exact)).astype(np.int32)
    large = np.minimum(large, REL_BUCKETS - 1)
    return np.where(d < max_exact, d, large).astype(np.int32)


def _swa_bias(rel_bias):
    W = WINDOW
    pairs = SW_GROUP // 2
    dist = np.arange(W)[:, None] + W - np.arange(2 * W)[None, :]
    in_window = (dist >= 0) & (dist < W)
    onehot = (_t5_bucket(dist).reshape(-1)[None, :] == np.arange(REL_BUCKETS)[:, None]).astype(np.float32)
    bias = jnp.dot(rel_bias.astype(F32).T, onehot, precision=jax.lax.Precision.HIGHEST)
    bias = bias.reshape(SW_KV_HEADS, pairs, 2, W, 2 * W)
    bias = jnp.transpose(bias, (0, 2, 1, 3, 4)).reshape(SW_KV_HEADS, 2, pairs * W, 2 * W)
    mask_later = np.tile(in_window, (pairs, 1))
    mask_first = np.tile(in_window & (np.arange(2 * W) >= W)[None, :], (pairs, 1))
    return jnp.stack([jnp.where(mask_first, bias, -jnp.inf), jnp.where(mask_later, bias, -jnp.inf)])


def kernel(x, p, ffn1_norm, ffn1_wi, ffn1_wo, mix_norm, w_in, b_igate, b_fgate, ml_out_norm, q_norm, k_norm, sinks, rel_bias, w_a, w_b, w_out, ffn2_norm, ffn2_wi, ffn2_wo, ple_norm, w_ple_gate, w_ple):
    B, S, _ = x.shape
    depth = p.shape[0]
    n_tok = B * S
    tm = min(TOKEN_TILE, S)
    offs = [0] + [int(o) for o in np.cumsum(SPLITS)]

    bf = lambda w: w.astype(BF16)
    rows = lambda v: v.reshape(depth, 1, -1).astype(F32)
    w_in_b = bf(w_in)
    col = lambda a, b: w_in_b[:, :, offs[a]:offs[b]]
    wm = jnp.concatenate([col(0, 1), col(2, 4)], axis=2)
    wkt = jnp.swapaxes(col(1, 2), 1, 2)
    wift = jnp.swapaxes(col(4, 6), 1, 2)
    ws = col(6, 9)
    wg = col(9, 11)
    qk_gain = jnp.concatenate([jnp.tile(q_norm, (1, SW_Q_HEADS)) * (SW_HEAD_DIM ** -0.5),
                               jnp.tile(k_norm, (1, SW_KV_HEADS))], axis=1).reshape(depth, 1, -1).astype(F32)
    gate_bias = jnp.concatenate([b_igate, b_fgate], axis=1).reshape(depth, -1, 1).astype(F32)
    sink = jnp.broadcast_to(sinks.astype(F32)[:, :, None], (depth, SW_Q_HEADS, V7X_LANES))
    ml_gain = ml_out_norm.astype(F32)
    f1g, f1wi, f1wo = rows(ffn1_norm), bf(ffn1_wi), bf(ffn1_wo)
    f2g, f2wi, f2wo = rows(ffn2_norm), bf(ffn2_wi), bf(ffn2_wo)
    mixg, pleg = rows(mix_norm), rows(ple_norm)
    wa, wb, wo, wpg, wpp = bf(w_a), bf(w_b), bf(w_out), bf(w_ple_gate), bf(w_ple)
    p3 = p.reshape(depth, n_tok, P_DIM)

    bias2 = _swa_bias(rel_bias)
    head_of = np.arange(SW_Q_W + SW_KV_W) // SW_HEAD_DIM
    group_ones = jnp.asarray(head_of[:, None] == head_of[None, :], BF16)

    flat = lambda a: a.reshape(n_tok, a.shape[-1])
    x2 = x.reshape(n_tok, D_MODEL)
    for i in range(depth):
        x2 = _ffn(x2, f1g, f1wi, f1wo, i, tm)
        tok_major, gate, transposed, gate_rows = _inproj(
            x2.reshape(B, S, D_MODEL), mixg, wm, wkt, wift, gate_bias, ws, wg, qk_gain, group_ones, i, tm)
        x2 = _mixpost(x2, flat(gate), p3, tok_major, transposed, gate_rows, bias2, sink, ml_gain,
                      wa, wb, wo, f2g, f2wi, f2wo, pleg, wpg, wpp, i, S, tm)
    return x2.reshape(B, S, D_MODEL)
```

```python
import functools
import math

import jax
import jax.numpy as jnp
import numpy as np
from jax.experimental import pallas as pl
from jax.experimental.pallas import tpu as pltpu

D_MODEL = 1024
P_DIM = 256
D_FF = 2816
ML_HEADS = 4
ML_QK_DIM = 64
ML_V_DIM = 128
SW_Q_HEADS = 8
SW_KV_HEADS = 2
SW_HEAD_DIM = 64
WINDOW = 128
REL_BUCKETS = 32
REL_MAX_DIST = 128
EPS = 1e-6

ML_QK_W = ML_HEADS * ML_QK_DIM
ML_V_W = ML_HEADS * ML_V_DIM
SW_Q_W = SW_Q_HEADS * SW_HEAD_DIM
SW_KV_W = SW_KV_HEADS * SW_HEAD_DIM
SW_GROUP = SW_Q_HEADS // SW_KV_HEADS
SPLITS = (ML_QK_W, ML_QK_W, ML_V_W, ML_V_W, ML_HEADS, ML_HEADS, SW_Q_W, SW_KV_W, SW_KV_W, D_MODEL, D_MODEL)

TOK_MV = ML_QK_W
TOK_MO = TOK_MV + ML_V_W
TOK_SQ = TOK_MO + ML_V_W
TOK_SV = TOK_SQ + SW_Q_W
TOK_W = TOK_SV + SW_KV_W
TR_W = ML_QK_W + SW_KV_W

V7X_LANES = 128
V7X_VMEM_BYTES = 64 * 1024 * 1024
V7X_VMEM_LIMIT_BYTES = V7X_VMEM_BYTES - 4 * 1024 * 1024

FFN_CHUNK = 256
ML_CHUNK = 128
TOKEN_TILE = 512
FFN_TOKEN_TILE = 1024

F32 = jnp.float32
BF16 = jnp.bfloat16


def _dot(a, b):
    return jnp.dot(a, b, preferred_element_type=F32)


def lax_rsqrt(v):
    return jax.lax.rsqrt(v)


def _rms_scale(xf):
    return lax_rsqrt(jnp.mean(xf * xf, axis=-1, keepdims=True) + EPS)


def _sigmoid(v):
    return 1.0 / (1.0 + jnp.exp(-v))


def _const_spec(shape):
    nd = len(shape)
    return pl.BlockSpec(shape, lambda *_: (0,) * nd, pipeline_mode=pl.Buffered(1))


def _layer_spec(stacked, layer):
    tail = stacked.shape[1:]
    return pl.BlockSpec((None,) + tail, lambda *_: (layer,) + (0,) * len(tail), pipeline_mode=pl.Buffered(1))


def _params(*sem, flags=None):
    return pltpu.CompilerParams(dimension_semantics=sem, vmem_limit_bytes=V7X_VMEM_LIMIT_BYTES, flags=flags)


def _ffn_half_step(xf, g_ref, wi_ref, wo_ref, n_ref, act_ref):
    n_ref[...] = (xf * _rms_scale(xf) * g_ref[...]).astype(BF16)
    for j in range(D_FF // FFN_CHUNK):
        lo = j * FFN_CHUNK
        hg = _dot(n_ref[...], wi_ref[:, lo:lo + FFN_CHUNK])
        hu = _dot(n_ref[...], wi_ref[:, D_FF + lo:D_FF + lo + FFN_CHUNK])
        act_ref[:, lo:lo + FFN_CHUNK] = (hg * _sigmoid(hg) * hu).astype(BF16)
    return xf + 0.5 * _dot(act_ref[...], wo_ref[...])


def _ffn_body(x_ref, g_ref, wi_ref, wo_ref, o_ref, n_ref, act_ref):
    o_ref[...] = _ffn_half_step(x_ref[...], g_ref, wi_ref, wo_ref, n_ref, act_ref)


def _ffn(x2, g, wi, wo, layer, tm):
    n_tok = x2.shape[0]
    return pl.pallas_call(
        _ffn_body,
        grid=(n_tok // tm,),
        in_specs=[pl.BlockSpec((tm, D_MODEL), lambda i: (i, 0)),
                  _layer_spec(g, layer), _layer_spec(wi, layer), _layer_spec(wo, layer)],
        out_specs=pl.BlockSpec((tm, D_MODEL), lambda i: (i, 0)),
        out_shape=jax.ShapeDtypeStruct((n_tok, D_MODEL), F32),
        scratch_shapes=[pltpu.VMEM((tm, D_MODEL), BF16), pltpu.VMEM((tm, D_FF), BF16)],
        compiler_params=_params("parallel"),
        name="ffn",
    )(x2, g, wi, wo)


def _inproj_body(x_ref, g_ref, wm_ref, wkt_ref, wift_ref, bif_ref, ws_ref, wg_ref, gain_ref, ones_ref,
                 tok_ref, gate_ref, tr_ref, gt_ref):
    xf = x_ref[...]
    u = (xf * _rms_scale(xf) * g_ref[...]).astype(BF16)
    tok_ref[:, :TOK_SQ] = _dot(u, wm_ref[...]).astype(BF16)
    nt = (((1,), (1,)), ((), ()))
    tr_ref[:ML_QK_W, :] = (jax.lax.dot_general(wkt_ref[...], u, nt, preferred_element_type=F32)
                           * (1.0 / math.sqrt(ML_QK_DIM))).astype(BF16)
    pre = jax.lax.dot_general(wift_ref[...], u, nt, preferred_element_type=F32) + bif_ref[...]
    logf = jnp.minimum(pre, 0.0) - jnp.log1p(jnp.exp(-jnp.abs(pre)))
    pos = jax.lax.broadcasted_iota(jnp.int32, pre.shape, 1) & (ML_CHUNK - 1)
    csum = logf
    k = 1
    while k < ML_CHUNK:
        csum = csum + jnp.where(pos >= k, pltpu.roll(csum, k, 1), 0.0)
        k *= 2
    gt_ref[:2 * ML_HEADS, :] = jnp.where(jax.lax.broadcasted_iota(jnp.int32, pre.shape, 0) < ML_HEADS, pre, logf)
    gt_ref[2 * ML_HEADS:, :] = csum
    zs = _dot(u, ws_ref[...])
    qk = zs[:, :SW_Q_W + SW_KV_W]
    ssq = _dot((qk * qk).astype(BF16), ones_ref[...])
    qkn = qk * lax_rsqrt(ssq * (1.0 / SW_HEAD_DIM) + EPS) * gain_ref[...]
    tok_ref[:, TOK_SQ:TOK_SV] = qkn[:, :SW_Q_W].astype(BF16)
    tr_ref[ML_QK_W:, :] = qkn[:, SW_Q_W:].T.astype(BF16)
    tok_ref[:, TOK_SV:] = zs[:, SW_Q_W + SW_KV_W:].astype(BF16)
    gate_ref[...] = _dot(u, wg_ref[...]).astype(BF16)


def _inproj(x3, g, wm, wkt, wift, bif, ws, wg, gain, ones, layer, tm):
    b, s, _ = x3.shape
    assert tm % ML_CHUNK == 0
    tok = lambda w: pl.BlockSpec((None, tm, w), lambda i, j: (i, j, 0))
    tok_t = lambda w: pl.BlockSpec((None, w, tm), lambda i, j: (i, 0, j))
    sd = jax.ShapeDtypeStruct
    return pl.pallas_call(
        _inproj_body,
        grid=(b, s // tm),
        in_specs=[tok(D_MODEL)] + [_layer_spec(a, layer) for a in (g, wm, wkt, wift, bif, ws, wg, gain)]
                 + [_const_spec(ones.shape)],
        out_specs=[tok(TOK_W), tok(2 * D_MODEL), tok_t(TR_W), tok_t(4 * ML_HEADS)],
        out_shape=[sd((b, s, TOK_W), BF16), sd((b, s, 2 * D_MODEL), BF16), sd((b, TR_W, s), BF16),
                   sd((b, 4 * ML_HEADS, s), F32)],
        compiler_params=_params("parallel", "parallel"),
        name="inproj",
    )(x3, g, wm, wkt, wift, bif, ws, wg, gain, ones)


def _mlstm_pieces(q_ref, kt_ref, v_ref, o_ref, gl_ref, gc_ref, gain_ref, h_ref, c_ref, m_ref):
    L = ML_CHUNK
    H, dk, dv = ML_HEADS, ML_QK_DIM, ML_V_DIM
    heads = range(H)
    t_idx = jax.lax.broadcasted_iota(jnp.int32, (L, H * L), 0)
    s_idx = jax.lax.broadcasted_iota(jnp.int32, (L, H * L), 1)
    causal = (s_idx & (L - 1)) <= t_idx
    ones_col = (jax.lax.broadcasted_iota(jnp.int32, (L, V7X_LANES), 1) == 0).astype(BF16)
    r_blk = jax.lax.broadcasted_iota(jnp.int32, (H * dk, H * L), 0) // dk
    c_blk = jax.lax.broadcasted_iota(jnp.int32, (H * dk, H * L), 1) // L
    zeros_c = jnp.zeros((dk, 2 * dv), BF16)
    zeros_k = jnp.zeros((dk, 2 * dk), BF16)

    def chunk(c):
        sl = slice(c * L, (c + 1) * L)
        st = {}

        def gates():
            gl = gl_ref[:, sl]
            gc = gc_ref[:, sl]
            b_rows = [gc[H + h:H + h + 1, :] for h in heads]
            c_rows = [gl[h:h + 1, :] - b_rows[h] for h in heads]
            st["m_prev"] = [m_ref[h:h + 1, 0:1] for h in heads]
            st["g_tot"] = [b_rows[h][:, L - 1:L] for h in heads]
            st["a_row"] = [st["g_tot"][h] + c_rows[h] for h in heads]
            st["m_new"] = [jnp.maximum(st["g_tot"][h] + st["m_prev"][h],
                                       jnp.max(st["a_row"][h], axis=-1, keepdims=True)) for h in heads]
            st["dm"] = jnp.where(causal, jnp.concatenate(c_rows, axis=1), -jnp.inf)
            st["bm"] = jnp.where(causal, jnp.concatenate([gl[H + h:H + h + 1, :] for h in heads], axis=1), 0.0)

        def scores():
            k_bd = jnp.where(r_blk == c_blk, jnp.concatenate([kt_ref[:, sl]] * H, axis=1), jnp.zeros((), BF16))
            st["s_all"] = _dot(q_ref[sl, :], k_bd)

        def weights():
            dm, bm = st["dm"], st["bm"]
            st["m_col"] = [jnp.maximum(jnp.max(dm[:, h * L:(h + 1) * L], axis=-1, keepdims=True), st["m_prev"][h])
                           for h in heads]
            st["b_col"] = [jnp.sum(bm[:, h * L:(h + 1) * L], axis=-1, keepdims=True) for h in heads]
            m_all = jnp.concatenate([jnp.broadcast_to(st["m_col"][h], (L, L)) for h in heads], axis=1)
            st["w_all"] = (jnp.exp(dm - m_all) * st["s_all"]).astype(BF16)
            st["res"] = [None] * H

        def head_matmul(h):
            def run():
                pair = h // 2
                v_aug = jnp.concatenate([v_ref[sl, h * dv:(h + 1) * dv], ones_col], axis=1)
                st.setdefault("c_aug", {})[h] = c_ref[h]
                c_bf = st["c_aug"][h].astype(BF16)
                rhs = jnp.concatenate([v_aug] + ([c_bf, zeros_c] if h % 2 == 0 else [zeros_c, c_bf]), axis=0)
                s_inter = jnp.exp(st["m_prev"][h] - st["m_col"][h])
                q_s = (q_ref[sl, pair * 2 * dk:(pair + 1) * 2 * dk].astype(F32) * s_inter).astype(BF16)
                kw = (kt_ref[h * dk:(h + 1) * dk, sl].astype(F32)
                      * jnp.exp(st["a_row"][h] - st["m_new"][h])).astype(BF16)
                lhs = jnp.concatenate([jnp.concatenate([st["w_all"][:, h * L:(h + 1) * L], q_s], axis=1),
                                       jnp.concatenate([kw, zeros_k], axis=1)], axis=0)
                st["res"][h] = _dot(lhs, rhs)
            return run

        def update():
            for h in heads:
                decay = jnp.exp(st["g_tot"][h] + st["m_prev"][h] - st["m_new"][h])
                c_ref[h] = decay * st["c_aug"][h] + st["res"][h][L:, :]
                m_ref[h:h + 1, :] = jnp.broadcast_to(st["m_new"][h], (1, V7X_LANES))

        def output(h):
            def run():
                res = st["res"][h]
                inv = 1.0 / jnp.maximum(jnp.abs(res[:L, dv:dv + 1]), jnp.exp(-(st["b_col"][h] + st["m_col"][h])))
                hh = res[:L, :dv] * inv
                hn = hh * lax_rsqrt(jnp.mean(hh * hh, axis=-1, keepdims=True) + EPS) * gain_ref[h:h + 1, :]
                gate = _sigmoid(o_ref[sl, h * dv:(h + 1) * dv].astype(F32))
                h_ref[sl, h * dv:(h + 1) * dv] = (gate * hn).astype(BF16)
            return run

        return ([gates, scores, weights] + [head_matmul(h) for h in heads] + [update]
                + [output(h) for h in heads])

    pieces = []
    for c in range(q_ref.shape[0] // L):
        pieces += chunk(c)
    return pieces


def _swa_pieces(q_ref, kt_ref, ktp_ref, v_ref, vp_ref, bias_ref, first_sel, sink_ref, o_ref):
    W, d = WINDOW, SW_HEAD_DIM
    pairs = SW_GROUP // 2
    rows = pairs * W
    groups = range(SW_KV_HEADS)
    lane = jax.lax.broadcasted_iota(jnp.int32, (rows, 2 * d), 1)
    zeros_kt = jnp.zeros((d, 2 * W), BF16)
    ones_v = jnp.ones((2 * W, 2 * d), BF16)

    def block(blk):
        cur = slice(blk * W, (blk + 1) * W)
        prev = slice((blk - 1) * W, blk * W)
        st = {}

        def scores():
            st["scores"], st["vv"] = [], []
            for g in groups:
                hd = slice(g * d, (g + 1) * d)
                kt_prev = ktp_ref[hd, :] if blk == 0 else kt_ref[hd, prev]
                v_prev = vp_ref[:, hd] if blk == 0 else v_ref[prev, hd]
                kt = jnp.concatenate([kt_prev, kt_ref[hd, cur]], axis=1)
                vg = jnp.concatenate([v_prev, v_ref[cur, hd]], axis=0)
                st["vv"].append(jnp.concatenate([vg, vg, ones_v], axis=1))
                kt_sel = jnp.concatenate([jnp.concatenate([kt, zeros_kt], axis=0),
                                          jnp.concatenate([zeros_kt, kt], axis=0)], axis=1)
                q_rows = jnp.concatenate([q_ref[cur, (g * pairs + pr) * 2 * d:(g * pairs + pr + 1) * 2 * d]
                                          for pr in range(pairs)], axis=0)
                st["scores"].append(_dot(q_rows, kt_sel))

        def softmax():
            st["probs"], st["tail"] = [], []
            for g in groups:
                for sub in range(2):
                    bias = bias_ref[first_sel, g, sub] if blk == 0 else bias_ref[1, g, sub]
                    logits = st["scores"][g][:, sub * 2 * W:(sub + 1) * 2 * W] + bias
                    sink = jnp.concatenate(
                        [jnp.broadcast_to(sink_ref[g * SW_GROUP + 2 * pr + sub:g * SW_GROUP + 2 * pr + sub + 1, :],
                                          (W, V7X_LANES)) for pr in range(pairs)], axis=0)
                    m = jnp.maximum(jnp.broadcast_to(jnp.max(logits, axis=-1, keepdims=True), (rows, V7X_LANES)),
                                    sink)
                    st["probs"].append(jnp.exp(logits - jnp.concatenate([m, m], axis=1)).astype(BF16))
                    st["tail"].append(jnp.exp(sink - m))

        def values():
            res = [_dot(st["probs"][2 * g + sub], st["vv"][g]) for g in groups for sub in range(2)]
            for g in groups:
                outs = [res[2 * g + sub][:, :2 * d] / (res[2 * g + sub][:, 2 * d:] + st["tail"][2 * g + sub])
                        for sub in range(2)]
                sel = jnp.where(lane < d, outs[0], outs[1]).astype(BF16)
                for pr in range(pairs):
                    o_ref[cur, (g * pairs + pr) * 2 * d:(g * pairs + pr + 1) * 2 * d] = sel[pr * W:(pr + 1) * W]

        return [scores, softmax, values]

    pieces = []
    for blk in range(q_ref.shape[0] // W):
        pieces += block(blk)
    return pieces


def _merge_lists(a, b):
    out, ia, ib = [], 0, 0
    while ia < len(a) or ib < len(b):
        if ib >= len(b) or (ia < len(a) and ia * len(b) <= ib * len(a)):
            out.append(a[ia]); ia += 1
        else:
            out.append(b[ib]); ib += 1
    return out


def _mixpost_body(x_ref, gate_ref, p_ref, tok_ref, tr_ref, gt_ref,
                  bias_ref, sink_ref, mlg_ref, wa_ref, wb_ref, wo_ref, fg_ref, fwi_ref, fwo_ref,
                  pg_ref, pwg_ref, pwp_ref, o_ref,
                  n_ref, act_ref, x1_ref, ha_ref, hb_ref, c_ref, m_ref, ktp_ref, vp_ref, *, tiles_per_seq):
    t = pl.program_id(0)
    n_tiles = pl.num_programs(0) - 1
    seq_start = jax.lax.rem(jnp.minimum(t, n_tiles - 1), tiles_per_seq) == 0

    @pl.when(seq_start)
    def _():
        c_ref[...] = jnp.zeros_like(c_ref)
        m_ref[...] = jnp.zeros_like(m_ref)
        ktp_ref[...] = jnp.zeros_like(ktp_ref)
        vp_ref[...] = jnp.zeros_like(vp_ref)

    first_sel = jnp.where(seq_start, 0, 1)
    cols = lambda lo, hi: tok_ref.at[:, lo:hi]
    mq_ref, mv_ref, mo_ref = cols(0, TOK_MV), cols(TOK_MV, TOK_MO), cols(TOK_MO, TOK_SQ)
    sq_ref, sv_ref = cols(TOK_SQ, TOK_SV), cols(TOK_SV, TOK_W)
    mkt_ref, skt_ref = tr_ref.at[:ML_QK_W, :], tr_ref.at[ML_QK_W:, :]
    gl_ref, gc_ref = gt_ref.at[:2 * ML_HEADS, :], gt_ref.at[2 * ML_HEADS:, :]

    def keep_last_block():
        ktp_ref[...] = skt_ref[:, skt_ref.shape[1] - WINDOW:]
        vp_ref[...] = sv_ref[sv_ref.shape[0] - WINDOW:, :]

    def mixer_pieces():
        return _merge_lists(
            _mlstm_pieces(mq_ref, mkt_ref, mv_ref, mo_ref, gl_ref, gc_ref, mlg_ref, ha_ref, c_ref, m_ref),
            _swa_pieces(sq_ref, skt_ref, ktp_ref, sv_ref, vp_ref, bias_ref, first_sel, sink_ref, hb_ref)
        ) + [keep_last_block]

    def merge_gated():
        ya = _dot(ha_ref[...], wa_ref[...])
        yb = _dot(hb_ref[...], wb_ref[...])
        return (_sigmoid(gate_ref[:, :D_MODEL].astype(F32)) * ya
                + _sigmoid(gate_ref[:, D_MODEL:].astype(F32)) * yb).astype(BF16)

    def carry(x1):
        x1_ref[...] = x1
        n_ref[...] = (x1 * _rms_scale(x1) * fg_ref[...]).astype(BF16)

    @pl.when(t == 0)
    def _():
        for piece in mixer_pieces():
            piece()
        carry(x_ref[...] + _dot(merge_gated(), wo_ref[...]))

    @pl.when(t > 0)
    def _():
        pieces = mixer_pieces()
        n_chunks = D_FF // FFN_CHUNK
        for j in range(n_chunks):
            lo = j * FFN_CHUNK
            hg = _dot(n_ref[...], fwi_ref[:, lo:lo + FFN_CHUNK])
            hu = _dot(n_ref[...], fwi_ref[:, D_FF + lo:D_FF + lo + FFN_CHUNK])
            act_ref[:, lo:lo + FFN_CHUNK] = (hg * _sigmoid(hg) * hu).astype(BF16)
            for piece in pieces[j * len(pieces) // n_chunks:(j + 1) * len(pieces) // n_chunks]:
                piece()
        x2 = x1_ref[...] + 0.5 * _dot(act_ref[...], fwo_ref[...])
        mixed = merge_gated()
        n_ple = (x2 * _rms_scale(x2) * pg_ref[...]).astype(BF16)
        x1 = x_ref[...] + _dot(mixed, wo_ref[...])
        gate = _sigmoid(_dot(n_ple, pwg_ref[...]))
        o_ref[...] = x2 + gate * _dot(p_ref[...].astype(BF16), pwp_ref[...])
        carry(x1)


def _mixpost(x2, gate, p3, tok_major, transposed, gate_rows, bias2, sink, ml_gain,
             wa, wb, wo, fg, fwi, fwo, pg, pwg, pwp, layer, seq_len, tm):
    n_tok = x2.shape[0]
    n_tiles = n_tok // tm
    tps = seq_len // tm
    assert tm % WINDOW == 0 and tm % ML_CHUNK == 0 and seq_len % tm == 0
    mix = lambda t: jnp.minimum(t, n_tiles - 1)
    post = lambda t: jnp.maximum(t - 1, 0)
    tok = lambda w: pl.BlockSpec((tm, w), lambda t: (post(t), 0))
    tok_mix = lambda w: pl.BlockSpec((tm, w), lambda t: (mix(t), 0))
    seq = lambda w: pl.BlockSpec((None, tm, w), lambda t: (mix(t) // tps, mix(t) % tps, 0))
    seq_t = lambda w: pl.BlockSpec((None, w, tm), lambda t: (mix(t) // tps, 0, mix(t) % tps))
    weights = (wa, wb, wo, fg, fwi, fwo, pg, pwg, pwp)
    return pl.pallas_call(
        functools.partial(_mixpost_body, tiles_per_seq=tps),
        grid=(n_tiles + 1,),
        in_specs=[tok_mix(D_MODEL), tok_mix(2 * D_MODEL),
                  pl.BlockSpec((None, tm, P_DIM), lambda t: (layer, post(t), 0)),
                  seq(TOK_W), seq_t(TR_W), seq_t(4 * ML_HEADS),
                  _const_spec(bias2.shape), _layer_spec(sink, layer), _layer_spec(ml_gain, layer)]
                 + [_layer_spec(w, layer) for w in weights],
        out_specs=tok(D_MODEL),
        out_shape=jax.ShapeDtypeStruct((n_tok, D_MODEL), F32),
        scratch_shapes=[pltpu.VMEM((tm, D_MODEL), BF16), pltpu.VMEM((tm, D_FF), BF16),
                        pltpu.VMEM((tm, D_MODEL), F32),
                        pltpu.VMEM((tm, ML_V_W), BF16), pltpu.VMEM((tm, SW_Q_W), BF16),
                        pltpu.VMEM((ML_HEADS, ML_QK_DIM, 2 * ML_V_DIM), F32),
                        pltpu.VMEM((2 * ML_HEADS, V7X_LANES), F32),
                        pltpu.VMEM((SW_KV_W, WINDOW), BF16), pltpu.VMEM((WINDOW, SW_KV_W), BF16)],
        compiler_params=_params("arbitrary"),
        name="mixpost",
    )(x2, gate, p3, tok_major, transposed, gate_rows, bias2, sink, ml_gain, *weights)


def _t5_bucket(dist):
    max_exact = REL_BUCKETS // 2
    d = np.maximum(dist, 0)
    large = max_exact + (np.log(np.maximum(d, 1) / max_exact) / np.log(REL_MAX_DIST / max_exact)
                         * (REL_BUCKETS - max_exact)).astype(np.int32)
    large = np.minimum(large, REL_BUCKETS - 1)
    return np.where(d < max_exact, d, large).astype(np.int32)


def _swa_bias(rel_bias):
    W = WINDOW
    pairs = SW_GROUP // 2
    dist = np.arange(W)[:, None] + W - np.arange(2 * W)[None, :]
    in_window = (dist >= 0) & (dist < W)
    onehot = (_t5_bucket(dist).reshape(-1)[None, :] == np.arange(REL_BUCKETS)[:, None]).astype(np.float32)
    bias = jnp.dot(rel_bias.astype(F32).T, onehot, precision=jax.lax.Precision.HIGHEST)
    bias = bias.reshape(SW_KV_HEADS, pairs, 2, W, 2 * W)
    bias = jnp.transpose(bias, (0, 2, 1, 3, 4)).reshape(SW_KV_HEADS, 2, pairs * W, 2 * W)
    mask_later = np.tile(in_window, (pairs, 1))
    mask_first = np.tile(in_window & (np.arange(2 * W) >= W)[None, :], (pairs, 1))
    return jnp.stack([jnp.where(mask_first, bias, -jnp.inf), jnp.where(mask_later, bias, -jnp.inf)])


def kernel(x, p, ffn1_norm, ffn1_wi, ffn1_wo, mix_norm, w_in, b_igate, b_fgate, ml_out_norm, q_norm, k_norm, sinks, rel_bias, w_a, w_b, w_out, ffn2_norm, ffn2_wi, ffn2_wo, ple_norm, w_ple_gate, w_ple):
    B, S, _ = x.shape
    depth = p.shape[0]
    n_tok = B * S
    tm = min(TOKEN_TILE, S)
    offs = [0] + [int(o) for o in np.cumsum(SPLITS)]

    bf = lambda w: w.astype(BF16)
    rows = lambda v: v.reshape(depth, 1, -1).astype(F32)
    w_in_b = bf(w_in)
    col = lambda a, b: w_in_b[:, :, offs[a]:offs[b]]
    wm = jnp.concatenate([col(0, 1), col(2, 4)], axis=2)
    wkt = jnp.swapaxes(col(1, 2), 1, 2)
    wift = jnp.swapaxes(col(4, 6), 1, 2)
    ws = col(6, 9)
    wg = col(9, 11)
    qk_gain = jnp.concatenate([jnp.tile(q_norm, (1, SW_Q_HEADS)) * (SW_HEAD_DIM ** -0.5),
                               jnp.tile(k_norm, (1, SW_KV_HEADS))], axis=1).reshape(depth, 1, -1).astype(F32)
    gate_bias = jnp.concatenate([b_igate, b_fgate], axis=1).reshape(depth, -1, 1).astype(F32)
    sink = jnp.broadcast_to(sinks.astype(F32)[:, :, None], (depth, SW_Q_HEADS, V7X_LANES))
    ml_gain = ml_out_norm.astype(F32)
    f1g, f1wi, f1wo = rows(ffn1_norm), bf(ffn1_wi), bf(ffn1_wo)
    f2g, f2wi, f2wo = rows(ffn2_norm), bf(ffn2_wi), bf(ffn2_wo)
    mixg, pleg = rows(mix_norm), rows(ple_norm)
    wa, wb, wo, wpg, wpp = bf(w_a), bf(w_b), bf(w_out), bf(w_ple_gate), bf(w_ple)
    p3 = p.reshape(depth, n_tok, P_DIM)

    bias2 = _swa_bias(rel_bias)
    head_of = np.arange(SW_Q_W + SW_KV_W) // SW_HEAD_DIM
    group_ones = jnp.asarray(head_of[:, None] == head_of[None, :], BF16)

    flat = lambda a: a.reshape(n_tok, a.shape[-1])
    x2 = x.reshape(n_tok, D_MODEL)
    for i in range(depth):
        x2 = _ffn(x2, f1g, f1wi, f1wo, i, min(FFN_TOKEN_TILE, n_tok))
        tok_major, gate, transposed, gate_rows = _inproj(
            x2.reshape(B, S, D_MODEL), mixg, wm, wkt, wift, gate_bias, ws, wg, qk_gain, group_ones, i, tm)
        x2 = _mixpost(x2, flat(gate), p3, tok_major, transposed, gate_rows, bias2, sink, ml_gain,
                      wa, wb, wo, f2g, f2wi, f2wo, pleg, wpg, wpp, i, S, tm)
    return x2.reshape(B, S, D_MODEL)
```

```python
import functools
import math

import jax
import jax.numpy as jnp
import numpy as np
from jax.experimental import pallas as pl
from jax.experimental.pallas import tpu as pltpu

D_MODEL = 1024
P_DIM = 256
D_FF = 2816
ML_HEADS = 4
ML_QK_DIM = 64
ML_V_DIM = 128
SW_Q_HEADS = 8
SW_KV_HEADS = 2
SW_HEAD_DIM = 64
WINDOW = 128
REL_BUCKETS = 32
REL_MAX_DIST = 128
EPS = 1e-6

ML_QK_W = ML_HEADS * ML_QK_DIM
ML_V_W = ML_HEADS * ML_V_DIM
SW_Q_W = SW_Q_HEADS * SW_HEAD_DIM
SW_KV_W = SW_KV_HEADS * SW_HEAD_DIM
SW_GROUP = SW_Q_HEADS // SW_KV_HEADS
SPLITS = (ML_QK_W, ML_QK_W, ML_V_W, ML_V_W, ML_HEADS, ML_HEADS, SW_Q_W, SW_KV_W, SW_KV_W, D_MODEL, D_MODEL)

TOK_MV = ML_QK_W
TOK_MO = TOK_MV + ML_V_W
TOK_SQ = TOK_MO + ML_V_W
TOK_SV = TOK_SQ + SW_Q_W
TOK_W = TOK_SV + SW_KV_W
TR_W = ML_QK_W + SW_KV_W

V7X_LANES = 128
V7X_MXU_DIM = 256
V7X_BF16_SUBLANES = 16
V7X_VMEM_BYTES = 64 * 1024 * 1024
V7X_VMEM_LIMIT_BYTES = V7X_VMEM_BYTES - 4 * 1024 * 1024

FFN_CHUNK = 256
ML_CHUNK = 128
TOKEN_TILE = 512
FFN_TOKEN_TILE = 1024
INPROJ_TOKEN_TILE = 1024

F32 = jnp.float32
BF16 = jnp.bfloat16


def _dot(a, b):
    return jnp.dot(a, b, preferred_element_type=F32)


def lax_rsqrt(v):
    return jax.lax.rsqrt(v)


def _rms_scale(xf):
    return lax_rsqrt(jnp.mean(xf * xf, axis=-1, keepdims=True) + EPS)


def _sigmoid(v):
    return 1.0 / (1.0 + jnp.exp(-v))


def _const_spec(shape):
    nd = len(shape)
    return pl.BlockSpec(shape, lambda *_: (0,) * nd, pipeline_mode=pl.Buffered(1))


def _layer_spec(stacked, layer):
    tail = stacked.shape[1:]
    return pl.BlockSpec((None,) + tail, lambda *_: (layer,) + (0,) * len(tail), pipeline_mode=pl.Buffered(1))


def _params(*sem, flags=None):
    return pltpu.CompilerParams(dimension_semantics=sem, vmem_limit_bytes=V7X_VMEM_LIMIT_BYTES, flags=flags)


def _ffn_half_step(xf, g_ref, wi_ref, wo_ref, n_ref, act_ref):
    n_ref[...] = (xf * _rms_scale(xf) * g_ref[...]).astype(BF16)
    for j in range(D_FF // FFN_CHUNK):
        lo = j * FFN_CHUNK
        hg = _dot(n_ref[...], wi_ref[:, lo:lo + FFN_CHUNK])
        hu = _dot(n_ref[...], wi_ref[:, D_FF + lo:D_FF + lo + FFN_CHUNK])
        act_ref[:, lo:lo + FFN_CHUNK] = (hg * _sigmoid(hg) * hu).astype(BF16)
    return xf + 0.5 * _dot(act_ref[...], wo_ref[...])


def _ffn_body(x_ref, g_ref, wi_ref, wo_ref, o_ref, n_ref, act_ref):
    o_ref[...] = _ffn_half_step(x_ref[...], g_ref, wi_ref, wo_ref, n_ref, act_ref)


def _ffn(x2, g, wi, wo, layer, tm):
    n_tok = x2.shape[0]
    return pl.pallas_call(
        _ffn_body,
        grid=(n_tok // tm,),
        in_specs=[pl.BlockSpec((tm, D_MODEL), lambda i: (i, 0)),
                  _layer_spec(g, layer), _layer_spec(wi, layer), _layer_spec(wo, layer)],
        out_specs=pl.BlockSpec((tm, D_MODEL), lambda i: (i, 0)),
        out_shape=jax.ShapeDtypeStruct((n_tok, D_MODEL), F32),
        scratch_shapes=[pltpu.VMEM((tm, D_MODEL), BF16), pltpu.VMEM((tm, D_FF), BF16)],
        compiler_params=_params("parallel"),
        name="ffn",
    )(x2, g, wi, wo)


def _inproj_body(x_ref, g_ref, wm_ref, wt_ref, bif_ref, ws_ref, wg_ref, gain_ref, ones_ref,
                 tok_ref, gate_ref, tr_ref, gt_ref):
    xf = x_ref[...]
    u = (xf * _rms_scale(xf) * g_ref[...]).astype(BF16)
    tok_ref[:, :TOK_SQ] = _dot(u, wm_ref[...]).astype(BF16)
    nt = (((1,), (1,)), ((), ()))
    zt = jax.lax.dot_general(wt_ref[...], u, nt, preferred_element_type=F32)
    tr_ref[:ML_QK_W, :] = (zt[:ML_QK_W] * (1.0 / math.sqrt(ML_QK_DIM))).astype(BF16)
    pre = zt[ML_QK_W:ML_QK_W + 2 * ML_HEADS] + bif_ref[...]
    logf = jnp.minimum(pre, 0.0) - jnp.log1p(jnp.exp(-jnp.abs(pre)))
    pos = jax.lax.broadcasted_iota(jnp.int32, pre.shape, 1) & (ML_CHUNK - 1)
    csum = logf
    k = 1
    while k < ML_CHUNK:
        csum = csum + jnp.where(pos >= k, pltpu.roll(csum, k, 1), 0.0)
        k *= 2
    gt_ref[:2 * ML_HEADS, :] = jnp.where(jax.lax.broadcasted_iota(jnp.int32, pre.shape, 0) < ML_HEADS, pre, logf)
    gt_ref[2 * ML_HEADS:, :] = csum
    zs = _dot(u, ws_ref[...])
    qk = zs[:, :SW_Q_W + SW_KV_W]
    sq2 = (qk * qk).astype(BF16)
    slab = ones_ref.shape[0]
    widths = [min(slab, sq2.shape[1] - lo) for lo in range(0, sq2.shape[1], slab)]
    ssq = jnp.concatenate([_dot(sq2[:, i * slab:i * slab + w], ones_ref[:w, :w]) for i, w in enumerate(widths)],
                          axis=1)
    qkn = qk * lax_rsqrt(ssq * (1.0 / SW_HEAD_DIM) + EPS) * gain_ref[...]
    tok_ref[:, TOK_SQ:TOK_SV] = qkn[:, :SW_Q_W].astype(BF16)
    tr_ref[ML_QK_W:, :] = qkn[:, SW_Q_W:].T.astype(BF16)
    tok_ref[:, TOK_SV:] = zs[:, SW_Q_W + SW_KV_W:].astype(BF16)
    gate_ref[...] = _dot(u, wg_ref[...]).astype(BF16)


def _inproj(x3, g, wm, wt, bif, ws, wg, gain, ones, layer, tm):
    b, s, _ = x3.shape
    assert tm % ML_CHUNK == 0
    tok = lambda w: pl.BlockSpec((None, tm, w), lambda i, j: (i, j, 0))
    tok_t = lambda w: pl.BlockSpec((None, w, tm), lambda i, j: (i, 0, j))
    sd = jax.ShapeDtypeStruct
    return pl.pallas_call(
        _inproj_body,
        grid=(b, s // tm),
        in_specs=[tok(D_MODEL)] + [_layer_spec(a, layer) for a in (g, wm, wt, bif, ws, wg, gain)]
                 + [_const_spec(ones.shape)],
        out_specs=[tok(TOK_W), tok(2 * D_MODEL), tok_t(TR_W), tok_t(4 * ML_HEADS)],
        out_shape=[sd((b, s, TOK_W), BF16), sd((b, s, 2 * D_MODEL), BF16), sd((b, TR_W, s), BF16),
                   sd((b, 4 * ML_HEADS, s), F32)],
        compiler_params=_params("parallel", "parallel"),
        name="inproj",
    )(x3, g, wm, wt, bif, ws, wg, gain, ones)


def _mlstm_pieces(q_ref, kt_ref, v_ref, o_ref, gl_ref, gc_ref, gain_ref, h_ref, c_ref, m_ref):
    L = ML_CHUNK
    H, dk, dv = ML_HEADS, ML_QK_DIM, ML_V_DIM
    heads = range(H)
    t_idx = jax.lax.broadcasted_iota(jnp.int32, (L, H * L), 0)
    s_idx = jax.lax.broadcasted_iota(jnp.int32, (L, H * L), 1)
    causal = (s_idx & (L - 1)) <= t_idx
    ones_col = (jax.lax.broadcasted_iota(jnp.int32, (L, V7X_LANES), 1) == 0).astype(BF16)
    r_blk = jax.lax.broadcasted_iota(jnp.int32, (H * dk, H * L), 0) // dk
    c_blk = jax.lax.broadcasted_iota(jnp.int32, (H * dk, H * L), 1) // L
    zeros_c = jnp.zeros((dk, 2 * dv), BF16)
    zeros_k = jnp.zeros((dk, 2 * dk), BF16)

    def chunk(c):
        sl = slice(c * L, (c + 1) * L)
        st = {}

        def gates():
            gl = gl_ref[:, sl]
            gc = gc_ref[:, sl]
            b_rows = [gc[H + h:H + h + 1, :] for h in heads]
            c_rows = [gl[h:h + 1, :] - b_rows[h] for h in heads]
            st["m_prev"] = [m_ref[h:h + 1, 0:1] for h in heads]
            st["g_tot"] = [b_rows[h][:, L - 1:L] for h in heads]
            st["a_row"] = [st["g_tot"][h] + c_rows[h] for h in heads]
            st["m_new"] = [jnp.maximum(st["g_tot"][h] + st["m_prev"][h],
                                       jnp.max(st["a_row"][h], axis=-1, keepdims=True)) for h in heads]
            st["dm"] = jnp.where(causal, jnp.concatenate(c_rows, axis=1), -jnp.inf)
            st["bm"] = jnp.where(causal, jnp.concatenate([gl[H + h:H + h + 1, :] for h in heads], axis=1), 0.0)

        def scores():
            k_bd = jnp.where(r_blk == c_blk, jnp.concatenate([kt_ref[:, sl]] * H, axis=1), jnp.zeros((), BF16))
            st["s_all"] = _dot(q_ref[sl, :], k_bd)

        def weights():
            dm, bm = st["dm"], st["bm"]
            st["m_col"] = [jnp.maximum(jnp.max(dm[:, h * L:(h + 1) * L], axis=-1, keepdims=True), st["m_prev"][h])
                           for h in heads]
            st["b_col"] = [jnp.sum(bm[:, h * L:(h + 1) * L], axis=-1, keepdims=True) for h in heads]
            m_all = jnp.concatenate([jnp.broadcast_to(st["m_col"][h], (L, L)) for h in heads], axis=1)
            st["w_all"] = (jnp.exp(dm - m_all) * st["s_all"]).astype(BF16)
            st["res"] = [None] * H

        def head_matmul(h):
            def run():
                pair = h // 2
                v_aug = jnp.concatenate([v_ref[sl, h * dv:(h + 1) * dv], ones_col], axis=1)
                st.setdefault("c_aug", {})[h] = c_ref[h]
                c_bf = st["c_aug"][h].astype(BF16)
                rhs = jnp.concatenate([v_aug] + ([c_bf, zeros_c] if h % 2 == 0 else [zeros_c, c_bf]), axis=0)
                s_inter = jnp.exp(st["m_prev"][h] - st["m_col"][h])
                q_s = (q_ref[sl, pair * 2 * dk:(pair + 1) * 2 * dk].astype(F32) * s_inter).astype(BF16)
                kw = (kt_ref[h * dk:(h + 1) * dk, sl].astype(F32)
                      * jnp.exp(st["a_row"][h] - st["m_new"][h])).astype(BF16)
                lhs = jnp.concatenate([jnp.concatenate([st["w_all"][:, h * L:(h + 1) * L], q_s], axis=1),
                                       jnp.concatenate([kw, zeros_k], axis=1)], axis=0)
                st["res"][h] = _dot(lhs, rhs)
            return run

        def update():
            for h in heads:
                decay = jnp.exp(st["g_tot"][h] + st["m_prev"][h] - st["m_new"][h])
                c_ref[h] = decay * st["c_aug"][h] + st["res"][h][L:, :]
                m_ref[h:h + 1, :] = jnp.broadcast_to(st["m_new"][h], (1, V7X_LANES))

        def output(h):
            def run():
                res = st["res"][h]
                inv = 1.0 / jnp.maximum(jnp.abs(res[:L, dv:dv + 1]), jnp.exp(-(st["b_col"][h] + st["m_col"][h])))
                hh = res[:L, :dv] * inv
                hn = hh * lax_rsqrt(jnp.mean(hh * hh, axis=-1, keepdims=True) + EPS) * gain_ref[h:h + 1, :]
                gate = _sigmoid(o_ref[sl, h * dv:(h + 1) * dv].astype(F32))
                h_ref[sl, h * dv:(h + 1) * dv] = (gate * hn).astype(BF16)
            return run

        return ([gates, scores, weights] + [head_matmul(h) for h in heads] + [update]
                + [output(h) for h in heads])

    pieces = []
    for c in range(q_ref.shape[0] // L):
        pieces += chunk(c)
    return pieces


def _swa_pieces(q_ref, kt_ref, ktp_ref, v_ref, vp_ref, bias_ref, first_sel, sink_ref, o_ref):
    W, d = WINDOW, SW_HEAD_DIM
    pairs = SW_GROUP // 2
    rows = pairs * W
    groups = range(SW_KV_HEADS)
    lane = jax.lax.broadcasted_iota(jnp.int32, (rows, 2 * d), 1)
    zeros_kt = jnp.zeros((d, 2 * W), BF16)
    ones_v = jnp.ones((2 * W, 2 * d), BF16)

    def block(blk):
        cur = slice(blk * W, (blk + 1) * W)
        prev = slice((blk - 1) * W, blk * W)
        st = {}

        def scores():
            st["scores"], st["vv"] = [], []
            for g in groups:
                hd = slice(g * d, (g + 1) * d)
                kt_prev = ktp_ref[hd, :] if blk == 0 else kt_ref[hd, prev]
                v_prev = vp_ref[:, hd] if blk == 0 else v_ref[prev, hd]
                kt = jnp.concatenate([kt_prev, kt_ref[hd, cur]], axis=1)
                vg = jnp.concatenate([v_prev, v_ref[cur, hd]], axis=0)
                st["vv"].append(jnp.concatenate([vg, vg, ones_v], axis=1))
                kt_sel = jnp.concatenate([jnp.concatenate([kt, zeros_kt], axis=0),
                                          jnp.concatenate([zeros_kt, kt], axis=0)], axis=1)
                q_rows = jnp.concatenate([q_ref[cur, (g * pairs + pr) * 2 * d:(g * pairs + pr + 1) * 2 * d]
                                          for pr in range(pairs)], axis=0)
                st["scores"].append(_dot(q_rows, kt_sel))

        def softmax():
            st["probs"], st["tail"] = [], []
            for g in groups:
                for sub in range(2):
                    bias = bias_ref[first_sel, g, sub] if blk == 0 else bias_ref[1, g, sub]
                    logits = st["scores"][g][:, sub * 2 * W:(sub + 1) * 2 * W] + bias
                    sink = jnp.concatenate(
                        [jnp.broadcast_to(sink_ref[g * SW_GROUP + 2 * pr + sub:g * SW_GROUP + 2 * pr + sub + 1, :],
                                          (W, V7X_LANES)) for pr in range(pairs)], axis=0)
                    m = jnp.maximum(jnp.broadcast_to(jnp.max(logits, axis=-1, keepdims=True), (rows, V7X_LANES)),
                                    sink)
                    st["probs"].append(jnp.exp(logits - jnp.concatenate([m, m], axis=1)).astype(BF16))
                    st["tail"].append(jnp.exp(sink - m))

        def values():
            res = [_dot(st["probs"][2 * g + sub], st["vv"][g]) for g in groups for sub in range(2)]
            for g in groups:
                outs = [res[2 * g + sub][:, :2 * d] / (res[2 * g + sub][:, 2 * d:] + st["tail"][2 * g + sub])
                        for sub in range(2)]
                sel = jnp.where(lane < d, outs[0], outs[1]).astype(BF16)
                for pr in range(pairs):
                    o_ref[cur, (g * pairs + pr) * 2 * d:(g * pairs + pr + 1) * 2 * d] = sel[pr * W:(pr + 1) * W]

        return [scores, softmax, values]

    pieces = []
    for blk in range(q_ref.shape[0] // W):
        pieces += block(blk)
    return pieces


def _merge_lists(a, b):
    out, ia, ib = [], 0, 0
    while ia < len(a) or ib < len(b):
        if ib >= len(b) or (ia < len(a) and ia * len(b) <= ib * len(a)):
            out.append(a[ia]); ia += 1
        else:
            out.append(b[ib]); ib += 1
    return out


def _mixpost_body(x_ref, gate_ref, p_ref, tok_ref, tr_ref, gt_ref,
                  bias_ref, sink_ref, mlg_ref, wa_ref, wb_ref, wo_ref, fg_ref, fwi_ref, fwo_ref,
                  pg_ref, pwg_ref, pwp_ref, o_ref,
                  n_ref, act_ref, x1_ref, ha_ref, hb_ref, c_ref, m_ref, ktp_ref, vp_ref, *, tiles_per_seq):
    t = pl.program_id(0)
    n_tiles = pl.num_programs(0) - 1
    seq_start = jax.lax.rem(jnp.minimum(t, n_tiles - 1), tiles_per_seq) == 0

    @pl.when(seq_start)
    def _():
        c_ref[...] = jnp.zeros_like(c_ref)
        m_ref[...] = jnp.zeros_like(m_ref)
        ktp_ref[...] = jnp.zeros_like(ktp_ref)
        vp_ref[...] = jnp.zeros_like(vp_ref)

    first_sel = jnp.where(seq_start, 0, 1)
    cols = lambda lo, hi: tok_ref.at[:, lo:hi]
    mq_ref, mv_ref, mo_ref = cols(0, TOK_MV), cols(TOK_MV, TOK_MO), cols(TOK_MO, TOK_SQ)
    sq_ref, sv_ref = cols(TOK_SQ, TOK_SV), cols(TOK_SV, TOK_W)
    mkt_ref, skt_ref = tr_ref.at[:ML_QK_W, :], tr_ref.at[ML_QK_W:, :]
    gl_ref, gc_ref = gt_ref.at[:2 * ML_HEADS, :], gt_ref.at[2 * ML_HEADS:, :]

    def keep_last_block():
        ktp_ref[...] = skt_ref[:, skt_ref.shape[1] - WINDOW:]
        vp_ref[...] = sv_ref[sv_ref.shape[0] - WINDOW:, :]

    def mixer_pieces():
        return _merge_lists(
            _mlstm_pieces(mq_ref, mkt_ref, mv_ref, mo_ref, gl_ref, gc_ref, mlg_ref, ha_ref, c_ref, m_ref),
            _swa_pieces(sq_ref, skt_ref, ktp_ref, sv_ref, vp_ref, bias_ref, first_sel, sink_ref, hb_ref)
        ) + [keep_last_block]

    def merge_gated():
        ya = _dot(ha_ref[...], wa_ref[...])
        yb = _dot(hb_ref[...], wb_ref[...])
        return (_sigmoid(gate_ref[:, :D_MODEL].astype(F32)) * ya
                + _sigmoid(gate_ref[:, D_MODEL:].astype(F32)) * yb).astype(BF16)

    def carry(x1):
        x1_ref[...] = x1
        n_ref[...] = (x1 * _rms_scale(x1) * fg_ref[...]).astype(BF16)

    @pl.when(t == 0)
    def _():
        for piece in mixer_pieces():
            piece()
        carry(x_ref[...] + _dot(merge_gated(), wo_ref[...]))

    @pl.when(t > 0)
    def _():
        pieces = mixer_pieces()
        n_chunks = D_FF // FFN_CHUNK
        for j in range(n_chunks):
            lo = j * FFN_CHUNK
            hg = _dot(n_ref[...], fwi_ref[:, lo:lo + FFN_CHUNK])
            hu = _dot(n_ref[...], fwi_ref[:, D_FF + lo:D_FF + lo + FFN_CHUNK])
            act_ref[:, lo:lo + FFN_CHUNK] = (hg * _sigmoid(hg) * hu).astype(BF16)
            for piece in pieces[j * len(pieces) // n_chunks:(j + 1) * len(pieces) // n_chunks]:
                piece()
        x2 = x1_ref[...] + 0.5 * _dot(act_ref[...], fwo_ref[...])
        mixed = merge_gated()
        n_ple = (x2 * _rms_scale(x2) * pg_ref[...]).astype(BF16)
        x1 = x_ref[...] + _dot(mixed, wo_ref[...])
        gate = _sigmoid(_dot(n_ple, pwg_ref[...]))
        o_ref[...] = x2 + gate * _dot(p_ref[...].astype(BF16), pwp_ref[...])
        carry(x1)


def _mixpost(x2, gate, p3, tok_major, transposed, gate_rows, bias2, sink, ml_gain,
             wa, wb, wo, fg, fwi, fwo, pg, pwg, pwp, layer, seq_len, tm):
    n_tok = x2.shape[0]
    n_tiles = n_tok // tm
    tps = seq_len // tm
    assert tm % WINDOW == 0 and tm % ML_CHUNK == 0 and seq_len % tm == 0
    mix = lambda t: jnp.minimum(t, n_tiles - 1)
    post = lambda t: jnp.maximum(t - 1, 0)
    tok = lambda w: pl.BlockSpec((tm, w), lambda t: (post(t), 0))
    tok_mix = lambda w: pl.BlockSpec((tm, w), lambda t: (mix(t), 0))
    seq = lambda w: pl.BlockSpec((None, tm, w), lambda t: (mix(t) // tps, mix(t) % tps, 0))
    seq_t = lambda w: pl.BlockSpec((None, w, tm), lambda t: (mix(t) // tps, 0, mix(t) % tps))
    weights = (wa, wb, wo, fg, fwi, fwo, pg, pwg, pwp)
    return pl.pallas_call(
        functools.partial(_mixpost_body, tiles_per_seq=tps),
        grid=(n_tiles + 1,),
        in_specs=[tok_mix(D_MODEL), tok_mix(2 * D_MODEL),
                  pl.BlockSpec((None, tm, P_DIM), lambda t: (layer, post(t), 0)),
                  seq(TOK_W), seq_t(TR_W), seq_t(4 * ML_HEADS),
                  _const_spec(bias2.shape), _layer_spec(sink, layer), _layer_spec(ml_gain, layer)]
                 + [_layer_spec(w, layer) for w in weights],
        out_specs=tok(D_MODEL),
        out_shape=jax.ShapeDtypeStruct((n_tok, D_MODEL), F32),
        scratch_shapes=[pltpu.VMEM((tm, D_MODEL), BF16), pltpu.VMEM((tm, D_FF), BF16),
                        pltpu.VMEM((tm, D_MODEL), F32),
                        pltpu.VMEM((tm, ML_V_W), BF16), pltpu.VMEM((tm, SW_Q_W), BF16),
                        pltpu.VMEM((ML_HEADS, ML_QK_DIM, 2 * ML_V_DIM), F32),
                        pltpu.VMEM((2 * ML_HEADS, V7X_LANES), F32),
                        pltpu.VMEM((SW_KV_W, WINDOW), BF16), pltpu.VMEM((WINDOW, SW_KV_W), BF16)],
        compiler_params=_params("arbitrary"),
        name="mixpost",
    )(x2, gate, p3, tok_major, transposed, gate_rows, bias2, sink, ml_gain, *weights)


def _t5_bucket(dist):
    max_exact = REL_BUCKETS // 2
    d = np.maximum(dist, 0)
    large = max_exact + (np.log(np.maximum(d, 1) / max_exact) / np.log(REL_MAX_DIST / max_exact)
                         * (REL_BUCKETS - max_exact)).astype(np.int32)
    large = np.minimum(large, REL_BUCKETS - 1)
    return np.where(d < max_exact, d, large).astype(np.int32)


def _swa_bias(rel_bias):
    W = WINDOW
    pairs = SW_GROUP // 2
    dist = np.arange(W)[:, None] + W - np.arange(2 * W)[None, :]
    in_window = (dist >= 0) & (dist < W)
    onehot = (_t5_bucket(dist).reshape(-1)[None, :] == np.arange(REL_BUCKETS)[:, None]).astype(np.float32)
    bias = jnp.dot(rel_bias.astype(F32).T, onehot, precision=jax.lax.Precision.HIGHEST)
    bias = bias.reshape(SW_KV_HEADS, pairs, 2, W, 2 * W)
    bias = jnp.transpose(bias, (0, 2, 1, 3, 4)).reshape(SW_KV_HEADS, 2, pairs * W, 2 * W)
    mask_later = np.tile(in_window, (pairs, 1))
    mask_first = np.tile(in_window & (np.arange(2 * W) >= W)[None, :], (pairs, 1))
    return jnp.stack([jnp.where(mask_first, bias, -jnp.inf), jnp.where(mask_later, bias, -jnp.inf)])


def kernel(x, p, ffn1_norm, ffn1_wi, ffn1_wo, mix_norm, w_in, b_igate, b_fgate, ml_out_norm, q_norm, k_norm, sinks, rel_bias, w_a, w_b, w_out, ffn2_norm, ffn2_wi, ffn2_wo, ple_norm, w_ple_gate, w_ple):
    B, S, _ = x.shape
    depth = p.shape[0]
    n_tok = B * S
    tm = min(TOKEN_TILE, S)
    offs = [0] + [int(o) for o in np.cumsum(SPLITS)]

    bf = lambda w: w.astype(BF16)
    rows = lambda v: v.reshape(depth, 1, -1).astype(F32)
    w_in_b = bf(w_in)
    col = lambda a, b: w_in_b[:, :, offs[a]:offs[b]]
    wm = jnp.concatenate([col(0, 1), col(2, 4)], axis=2)
    wt = jnp.swapaxes(jnp.concatenate([col(1, 2), col(4, 6)], axis=2), 1, 2)
    wt = jnp.pad(wt, ((0, 0), (0, -wt.shape[1] % V7X_BF16_SUBLANES), (0, 0)))
    ws = col(6, 9)
    wg = col(9, 11)
    qk_gain = jnp.concatenate([jnp.tile(q_norm, (1, SW_Q_HEADS)) * (SW_HEAD_DIM ** -0.5),
                               jnp.tile(k_norm, (1, SW_KV_HEADS))], axis=1).reshape(depth, 1, -1).astype(F32)
    gate_bias = jnp.concatenate([b_igate, b_fgate], axis=1).reshape(depth, -1, 1).astype(F32)
    sink = jnp.broadcast_to(sinks.astype(F32)[:, :, None], (depth, SW_Q_HEADS, V7X_LANES))
    ml_gain = ml_out_norm.astype(F32)
    f1g, f1wi, f1wo = rows(ffn1_norm), bf(ffn1_wi), bf(ffn1_wo)
    f2g, f2wi, f2wo = rows(ffn2_norm), bf(ffn2_wi), bf(ffn2_wo)
    mixg, pleg = rows(mix_norm), rows(ple_norm)
    wa, wb, wo, wpg, wpp = bf(w_a), bf(w_b), bf(w_out), bf(w_ple_gate), bf(w_ple)
    p3 = p.reshape(depth, n_tok, P_DIM)

    bias2 = _swa_bias(rel_bias)
    head_of = np.arange(V7X_MXU_DIM) // SW_HEAD_DIM
    group_ones = jnp.asarray(head_of[:, None] == head_of[None, :], BF16)

    flat = lambda a: a.reshape(n_tok, a.shape[-1])
    x2 = x.reshape(n_tok, D_MODEL)
    for i in range(depth):
        x2 = _ffn(x2, f1g, f1wi, f1wo, i, min(FFN_TOKEN_TILE, n_tok))
        tok_major, gate, transposed, gate_rows = _inproj(
            x2.reshape(B, S, D_MODEL), mixg, wm, wt, gate_bias, ws, wg, qk_gain, group_ones, i,
            min(INPROJ_TOKEN_TILE, S))
        x2 = _mixpost(x2, flat(gate), p3, tok_major, transposed, gate_rows, bias2, sink, ml_gain,
                      wa, wb, wo, f2g, f2wi, f2wo, pleg, wpg, wpp, i, S, tm)
    return x2.reshape(B, S, D_MODEL)
```

```python
import functools
import math

import jax
import jax.numpy as jnp
import numpy as np
from jax.experimental import pallas as pl
from jax.experimental.pallas import tpu as pltpu

D_MODEL = 1024
P_DIM = 256
D_FF = 2816
ML_HEADS = 4
ML_QK_DIM = 64
ML_V_DIM = 128
SW_Q_HEADS = 8
SW_KV_HEADS = 2
SW_HEAD_DIM = 64
WINDOW = 128
REL_BUCKETS = 32
REL_MAX_DIST = 128
EPS = 1e-6

ML_QK_W = ML_HEADS * ML_QK_DIM
ML_V_W = ML_HEADS * ML_V_DIM
SW_Q_W = SW_Q_HEADS * SW_HEAD_DIM
SW_KV_W = SW_KV_HEADS * SW_HEAD_DIM
SW_GROUP = SW_Q_HEADS // SW_KV_HEADS
SPLITS = (ML_QK_W, ML_QK_W, ML_V_W, ML_V_W, ML_HEADS, ML_HEADS, SW_Q_W, SW_KV_W, SW_KV_W, D_MODEL, D_MODEL)

TOK_MV = ML_QK_W
TOK_MO = TOK_MV + ML_V_W
TOK_SQ = TOK_MO + ML_V_W
TOK_SV = TOK_SQ + SW_Q_W
TOK_W = TOK_SV + SW_KV_W
TR_W = ML_QK_W + SW_KV_W

V7X_LANES = 128
V7X_MXU_DIM = 256
V7X_BF16_SUBLANES = 16
V7X_VMEM_BYTES = 64 * 1024 * 1024
V7X_VMEM_LIMIT_BYTES = V7X_VMEM_BYTES - 4 * 1024 * 1024

FFN_CHUNK = 256
ML_CHUNK = 128
TOKEN_TILE = 512
FFN_TOKEN_TILE = 1024
INPROJ_TOKEN_TILE = 1024

F32 = jnp.float32
BF16 = jnp.bfloat16


def _dot(a, b):
    return jnp.dot(a, b, preferred_element_type=F32)


def lax_rsqrt(v):
    return jax.lax.rsqrt(v)


def _rms_scale(xf):
    return lax_rsqrt(jnp.mean(xf * xf, axis=-1, keepdims=True) + EPS)


def _sigmoid(v):
    return 1.0 / (1.0 + jnp.exp(-v))


def _const_spec(shape):
    nd = len(shape)
    return pl.BlockSpec(shape, lambda *_: (0,) * nd, pipeline_mode=pl.Buffered(1))


def _layer_spec(stacked, layer):
    tail = stacked.shape[1:]
    return pl.BlockSpec((None,) + tail, lambda *_: (layer,) + (0,) * len(tail), pipeline_mode=pl.Buffered(1))


def _params(*sem, flags=None):
    return pltpu.CompilerParams(dimension_semantics=sem, vmem_limit_bytes=V7X_VMEM_LIMIT_BYTES, flags=flags)


def _ffn_half_step(xf, g_ref, wi_ref, wo_ref, n_ref, act_ref):
    n_ref[...] = (xf * _rms_scale(xf) * g_ref[...]).astype(BF16)
    for j in range(D_FF // FFN_CHUNK):
        lo = j * FFN_CHUNK
        hg = _dot(n_ref[...], wi_ref[:, lo:lo + FFN_CHUNK])
        hu = _dot(n_ref[...], wi_ref[:, D_FF + lo:D_FF + lo + FFN_CHUNK])
        act_ref[:, lo:lo + FFN_CHUNK] = (hg * _sigmoid(hg) * hu).astype(BF16)
    return xf + 0.5 * _dot(act_ref[...], wo_ref[...])


def _ffn_body(x_ref, g_ref, wi_ref, wo_ref, o_ref, n_ref, act_ref):
    o_ref[...] = _ffn_half_step(x_ref[...], g_ref, wi_ref, wo_ref, n_ref, act_ref)


def _ffn(x2, g, wi, wo, layer, tm):
    n_tok = x2.shape[0]
    return pl.pallas_call(
        _ffn_body,
        grid=(n_tok // tm,),
        in_specs=[pl.BlockSpec((tm, D_MODEL), lambda i: (i, 0)),
                  _layer_spec(g, layer), _layer_spec(wi, layer), _layer_spec(wo, layer)],
        out_specs=pl.BlockSpec((tm, D_MODEL), lambda i: (i, 0)),
        out_shape=jax.ShapeDtypeStruct((n_tok, D_MODEL), F32),
        scratch_shapes=[pltpu.VMEM((tm, D_MODEL), BF16), pltpu.VMEM((tm, D_FF), BF16)],
        compiler_params=_params("parallel"),
        name="ffn",
    )(x2, g, wi, wo)


def _inproj_body(x_ref, g_ref, wm_ref, wt_ref, bif_ref, ws_ref, wg_ref, gain_ref, ones_ref,
                 tok_ref, gate_ref, tr_ref, gt_ref):
    xf = x_ref[...]
    u = (xf * _rms_scale(xf) * g_ref[...]).astype(BF16)
    tok_ref[:, :TOK_SQ] = _dot(u, wm_ref[...]).astype(BF16)
    nt = (((1,), (1,)), ((), ()))
    zt = jax.lax.dot_general(wt_ref[...], u, nt, preferred_element_type=F32)
    tr_ref[:ML_QK_W, :] = (zt[:ML_QK_W] * (1.0 / math.sqrt(ML_QK_DIM))).astype(BF16)
    pre = zt[ML_QK_W:ML_QK_W + 2 * ML_HEADS] + bif_ref[...]
    logf = jnp.minimum(pre, 0.0) - jnp.log1p(jnp.exp(-jnp.abs(pre)))
    pos = jax.lax.broadcasted_iota(jnp.int32, pre.shape, 1) & (ML_CHUNK - 1)
    csum = logf
    k = 1
    while k < ML_CHUNK:
        csum = csum + jnp.where(pos >= k, pltpu.roll(csum, k, 1), 0.0)
        k *= 2
    gt_ref[:2 * ML_HEADS, :] = jnp.where(jax.lax.broadcasted_iota(jnp.int32, pre.shape, 0) < ML_HEADS, pre, logf)
    gt_ref[2 * ML_HEADS:, :] = csum
    zs = _dot(u, ws_ref[...])
    qk = zs[:, :SW_Q_W + SW_KV_W]
    sq2 = (qk * qk).astype(BF16)
    slab = ones_ref.shape[0]
    widths = [min(slab, sq2.shape[1] - lo) for lo in range(0, sq2.shape[1], slab)]
    ssq = jnp.concatenate([_dot(sq2[:, i * slab:i * slab + w], ones_ref[:w, :w]) for i, w in enumerate(widths)],
                          axis=1)
    qkn = qk * lax_rsqrt(ssq * (1.0 / SW_HEAD_DIM) + EPS) * gain_ref[...]
    tok_ref[:, TOK_SQ:TOK_SV] = qkn[:, :SW_Q_W].astype(BF16)
    tr_ref[ML_QK_W:, :] = qkn[:, SW_Q_W:].T.astype(BF16)
    tok_ref[:, TOK_SV:] = zs[:, SW_Q_W + SW_KV_W:].astype(BF16)
    gate_ref[...] = _dot(u, wg_ref[...]).astype(BF16)


def _inproj(x3, g, wm, wt, bif, ws, wg, gain, ones, layer, tm):
    b, s, _ = x3.shape
    assert tm % ML_CHUNK == 0
    tok = lambda w: pl.BlockSpec((None, tm, w), lambda i, j: (i, j, 0))
    tok_t = lambda w: pl.BlockSpec((None, w, tm), lambda i, j: (i, 0, j))
    sd = jax.ShapeDtypeStruct
    return pl.pallas_call(
        _inproj_body,
        grid=(b, s // tm),
        in_specs=[tok(D_MODEL)] + [_layer_spec(a, layer) for a in (g, wm, wt, bif, ws, wg, gain)]
                 + [_const_spec(ones.shape)],
        out_specs=[tok(TOK_W), tok(2 * D_MODEL), tok_t(TR_W), tok_t(4 * ML_HEADS)],
        out_shape=[sd((b, s, TOK_W), BF16), sd((b, s, 2 * D_MODEL), BF16), sd((b, TR_W, s), BF16),
                   sd((b, 4 * ML_HEADS, s), F32)],
        compiler_params=_params("parallel", "parallel"),
        name="inproj",
    )(x3, g, wm, wt, bif, ws, wg, gain, ones)


def _mlstm_pieces(q_ref, kt_ref, v_ref, o_ref, gl_ref, gc_ref, gain_ref, h_ref, c_ref, m_ref):
    L = ML_CHUNK
    H, dk, dv = ML_HEADS, ML_QK_DIM, ML_V_DIM
    assert L == V7X_LANES and dv == V7X_LANES and 2 * dk == V7X_LANES and H % 2 == 0
    heads = range(H)
    t_idx = jax.lax.broadcasted_iota(jnp.int32, (L, H * L), 0)
    s_idx = jax.lax.broadcasted_iota(jnp.int32, (L, H * L), 1)
    causal = (s_idx & (L - 1)) <= t_idx
    ones_v = jnp.ones((L, V7X_LANES), BF16)
    r_blk = jax.lax.broadcasted_iota(jnp.int32, (H * dk, H * L), 0) // dk
    c_blk = jax.lax.broadcasted_iota(jnp.int32, (H * dk, H * L), 1) // L
    zeros_c = jnp.zeros((dk, 2 * dv), BF16)
    zeros_k = jnp.zeros((dk, 2 * dk), BF16)
    slab_ones = (jax.lax.broadcasted_iota(jnp.int32, (2 * V7X_LANES, 2 * V7X_LANES), 0) // V7X_LANES
                 == jax.lax.broadcasted_iota(jnp.int32, (2 * V7X_LANES, 2 * V7X_LANES), 1) // V7X_LANES).astype(BF16)

    def chunk(c):
        sl = slice(c * L, (c + 1) * L)
        st = {}

        def gates():
            gl = gl_ref[:, sl]
            gc = gc_ref[:, sl]
            b_rows = [gc[H + h:H + h + 1, :] for h in heads]
            c_rows = [gl[h:h + 1, :] - b_rows[h] for h in heads]
            st["m_prev"] = [m_ref[h:h + 1, 0:1] for h in heads]
            st["g_tot"] = [b_rows[h][:, L - 1:L] for h in heads]
            st["a_row"] = [st["g_tot"][h] + c_rows[h] for h in heads]
            st["m_new"] = [jnp.maximum(st["g_tot"][h] + st["m_prev"][h],
                                       jnp.max(st["a_row"][h], axis=-1, keepdims=True)) for h in heads]
            st["dm"] = jnp.where(causal, jnp.concatenate(c_rows, axis=1), -jnp.inf)
            st["bm"] = jnp.where(causal, jnp.concatenate([gl[H + h:H + h + 1, :] for h in heads], axis=1), 0.0)

        def scores():
            k_bd = jnp.where(r_blk == c_blk, jnp.concatenate([kt_ref[:, sl]] * H, axis=1), jnp.zeros((), BF16))
            st["s_all"] = _dot(q_ref[sl, :], k_bd)

        def weights():
            dm, bm = st["dm"], st["bm"].astype(BF16)
            st["m_rep"] = [jnp.broadcast_to(
                jnp.maximum(jnp.max(dm[:, h * L:(h + 1) * L], axis=-1, keepdims=True), st["m_prev"][h]),
                (L, V7X_LANES)) for h in heads]
            b_rep = [_dot(bm[:, pr * 2 * L:(pr + 1) * 2 * L], slab_ones) for pr in range(H // 2)]
            st["b_rep"] = [b_rep[h // 2][:, (h % 2) * L:(h % 2 + 1) * L] for h in heads]
            st["w_all"] = (jnp.exp(dm - jnp.concatenate(st["m_rep"], axis=1)) * st["s_all"]).astype(BF16)
            st["res"] = [None] * H

        def head_matmul(h):
            def run():
                pair = h // 2
                v_aug = jnp.concatenate([v_ref[sl, h * dv:(h + 1) * dv], ones_v], axis=1)
                st.setdefault("c_aug", {})[h] = c_ref[h]
                c_bf = st["c_aug"][h].astype(BF16)
                rhs = jnp.concatenate([v_aug] + ([c_bf, zeros_c] if h % 2 == 0 else [zeros_c, c_bf]), axis=0)
                s_inter = jnp.exp(st["m_prev"][h] - st["m_rep"][h])
                q_s = (q_ref[sl, pair * 2 * dk:(pair + 1) * 2 * dk].astype(F32) * s_inter).astype(BF16)
                kw = (kt_ref[h * dk:(h + 1) * dk, sl].astype(F32)
                      * jnp.exp(st["a_row"][h] - st["m_new"][h])).astype(BF16)
                lhs = jnp.concatenate([jnp.concatenate([st["w_all"][:, h * L:(h + 1) * L], q_s], axis=1),
                                       jnp.concatenate([kw, zeros_k], axis=1)], axis=0)
                st["res"][h] = _dot(lhs, rhs)
            return run

        def update():
            for h in heads:
                decay = jnp.exp(st["g_tot"][h] + st["m_prev"][h] - st["m_new"][h])
                c_ref[h] = decay * st["c_aug"][h] + st["res"][h][L:, :]
                m_ref[h:h + 1, :] = jnp.broadcast_to(st["m_new"][h], (1, V7X_LANES))

        def output(pair):
            def run():
                pair_heads = (2 * pair, 2 * pair + 1)
                hs = []
                for h in pair_heads:
                    res = st["res"][h]
                    floor = jnp.exp(-(st["b_rep"][h] + st["m_rep"][h]))
                    hs.append(res[:L, :dv] * (1.0 / jnp.maximum(jnp.abs(res[:L, dv:]), floor)))
                ssq = _dot(jnp.concatenate([hh * hh for hh in hs], axis=1).astype(BF16), slab_ones)
                for i, h in enumerate(pair_heads):
                    hn = hs[i] * lax_rsqrt(ssq[:, i * dv:(i + 1) * dv] * (1.0 / dv) + EPS) * gain_ref[h:h + 1, :]
                    gate = _sigmoid(o_ref[sl, h * dv:(h + 1) * dv].astype(F32))
                    h_ref[sl, h * dv:(h + 1) * dv] = (gate * hn).astype(BF16)
            return run

        return ([gates, scores, weights] + [head_matmul(h) for h in heads] + [update]
                + [output(pair) for pair in range(H // 2)])

    pieces = []
    for c in range(q_ref.shape[0] // L):
        pieces += chunk(c)
    return pieces


def _swa_pieces(q_ref, kt_ref, ktp_ref, v_ref, vp_ref, bias_ref, first_sel, sink_ref, o_ref):
    W, d = WINDOW, SW_HEAD_DIM
    pairs = SW_GROUP // 2
    rows = pairs * W
    groups = range(SW_KV_HEADS)
    lane = jax.lax.broadcasted_iota(jnp.int32, (rows, 2 * d), 1)
    zeros_kt = jnp.zeros((d, 2 * W), BF16)
    ones_v = jnp.ones((2 * W, 2 * d), BF16)

    def block(blk):
        cur = slice(blk * W, (blk + 1) * W)
        prev = slice((blk - 1) * W, blk * W)
        st = {}

        def scores():
            st["scores"], st["vv"] = [], []
            for g in groups:
                hd = slice(g * d, (g + 1) * d)
                kt_prev = ktp_ref[hd, :] if blk == 0 else kt_ref[hd, prev]
                v_prev = vp_ref[:, hd] if blk == 0 else v_ref[prev, hd]
                kt = jnp.concatenate([kt_prev, kt_ref[hd, cur]], axis=1)
                vg = jnp.concatenate([v_prev, v_ref[cur, hd]], axis=0)
                st["vv"].append(jnp.concatenate([vg, vg, ones_v], axis=1))
                kt_sel = jnp.concatenate([jnp.concatenate([kt, zeros_kt], axis=0),
                                          jnp.concatenate([zeros_kt, kt], axis=0)], axis=1)
                q_rows = jnp.concatenate([q_ref[cur, (g * pairs + pr) * 2 * d:(g * pairs + pr + 1) * 2 * d]
                                          for pr in range(pairs)], axis=0)
                st["scores"].append(_dot(q_rows, kt_sel))

        def softmax():
            st["probs"], st["tail"] = [], []
            for g in groups:
                for sub in range(2):
                    bias = bias_ref[first_sel, g, sub] if blk == 0 else bias_ref[1, g, sub]
                    logits = st["scores"][g][:, sub * 2 * W:(sub + 1) * 2 * W] + bias
                    sink = jnp.concatenate(
                        [jnp.broadcast_to(sink_ref[g * SW_GROUP + 2 * pr + sub:g * SW_GROUP + 2 * pr + sub + 1, :],
                                          (W, V7X_LANES)) for pr in range(pairs)], axis=0)
                    m = jnp.maximum(jnp.broadcast_to(jnp.max(logits, axis=-1, keepdims=True), (rows, V7X_LANES)),
                                    sink)
                    st["probs"].append(jnp.exp(logits - jnp.concatenate([m, m], axis=1)).astype(BF16))
                    st["tail"].append(jnp.exp(sink - m))

        def values():
            res = [_dot(st["probs"][2 * g + sub], st["vv"][g]) for g in groups for sub in range(2)]
            for g in groups:
                outs = [res[2 * g + sub][:, :2 * d] / (res[2 * g + sub][:, 2 * d:] + st["tail"][2 * g + sub])
                        for sub in range(2)]
                sel = jnp.where(lane < d, outs[0], outs[1]).astype(BF16)
                for pr in range(pairs):
                    o_ref[cur, (g * pairs + pr) * 2 * d:(g * pairs + pr + 1) * 2 * d] = sel[pr * W:(pr + 1) * W]

        return [scores, softmax, values]

    pieces = []
    for blk in range(q_ref.shape[0] // W):
        pieces += block(blk)
    return pieces


def _merge_lists(a, b):
    out, ia, ib = [], 0, 0
    while ia < len(a) or ib < len(b):
        if ib >= len(b) or (ia < len(a) and ia * len(b) <= ib * len(a)):
            out.append(a[ia]); ia += 1
        else:
            out.append(b[ib]); ib += 1
    return out


def _mixpost_body(x_ref, gate_ref, p_ref, tok_ref, tr_ref, gt_ref,
                  bias_ref, sink_ref, mlg_ref, wa_ref, wb_ref, wo_ref, fg_ref, fwi_ref, fwo_ref,
                  pg_ref, pwg_ref, pwp_ref, o_ref,
                  n_ref, act_ref, x1_ref, ha_ref, hb_ref, c_ref, m_ref, ktp_ref, vp_ref, *, tiles_per_seq):
    t = pl.program_id(0)
    n_tiles = pl.num_programs(0) - 1
    seq_start = jnp.minimum(t, n_tiles - 1) % tiles_per_seq == 0

    @pl.when(seq_start)
    def _():
        c_ref[...] = jnp.zeros_like(c_ref)
        m_ref[...] = jnp.zeros_like(m_ref)
        ktp_ref[...] = jnp.zeros_like(ktp_ref)
        vp_ref[...] = jnp.zeros_like(vp_ref)

    first_sel = jnp.where(seq_start, 0, 1)
    cols = lambda lo, hi: tok_ref.at[:, lo:hi]
    mq_ref, mv_ref, mo_ref = cols(0, TOK_MV), cols(TOK_MV, TOK_MO), cols(TOK_MO, TOK_SQ)
    sq_ref, sv_ref = cols(TOK_SQ, TOK_SV), cols(TOK_SV, TOK_W)
    mkt_ref, skt_ref = tr_ref.at[:ML_QK_W, :], tr_ref.at[ML_QK_W:, :]
    gl_ref, gc_ref = gt_ref.at[:2 * ML_HEADS, :], gt_ref.at[2 * ML_HEADS:, :]

    def keep_last_block():
        ktp_ref[...] = skt_ref[:, skt_ref.shape[1] - WINDOW:]
        vp_ref[...] = sv_ref[sv_ref.shape[0] - WINDOW:, :]

    def mixer_pieces():
        return _merge_lists(
            _mlstm_pieces(mq_ref, mkt_ref, mv_ref, mo_ref, gl_ref, gc_ref, mlg_ref, ha_ref, c_ref, m_ref),
            _swa_pieces(sq_ref, skt_ref, ktp_ref, sv_ref, vp_ref, bias_ref, first_sel, sink_ref, hb_ref)
        ) + [keep_last_block]

    def merge_gated():
        ya = _dot(ha_ref[...], wa_ref[...])
        yb = _dot(hb_ref[...], wb_ref[...])
        return (_sigmoid(gate_ref[:, :D_MODEL].astype(F32)) * ya
                + _sigmoid(gate_ref[:, D_MODEL:].astype(F32)) * yb).astype(BF16)

    def carry(x1):
        x1_ref[...] = x1
        n_ref[...] = (x1 * _rms_scale(x1) * fg_ref[...]).astype(BF16)

    @pl.when(t == 0)
    def _():
        for piece in mixer_pieces():
            piece()
        carry(x_ref[...] + _dot(merge_gated(), wo_ref[...]))

    @pl.when(t > 0)
    def _():
        pieces = mixer_pieces()
        n_chunks = D_FF // FFN_CHUNK
        for j in range(n_chunks):
            lo = j * FFN_CHUNK
            hg = _dot(n_ref[...], fwi_ref[:, lo:lo + FFN_CHUNK])
            hu = _dot(n_ref[...], fwi_ref[:, D_FF + lo:D_FF + lo + FFN_CHUNK])
            act_ref[:, lo:lo + FFN_CHUNK] = (hg * _sigmoid(hg) * hu).astype(BF16)
            for piece in pieces[j * len(pieces) // n_chunks:(j + 1) * len(pieces) // n_chunks]:
                piece()
        x2 = x1_ref[...] + 0.5 * _dot(act_ref[...], fwo_ref[...])
        mixed = merge_gated()
        n_ple = (x2 * _rms_scale(x2) * pg_ref[...]).astype(BF16)
        x1 = x_ref[...] + _dot(mixed, wo_ref[...])
        gate = _sigmoid(_dot(n_ple, pwg_ref[...]))
        o_ref[...] = x2 + gate * _dot(p_ref[...].astype(BF16), pwp_ref[...])
        carry(x1)


def _mixpost(x2, gate, p3, tok_major, transposed, gate_rows, bias2, sink, ml_gain,
             wa, wb, wo, fg, fwi, fwo, pg, pwg, pwp, layer, seq_len, tm):
    n_tok = x2.shape[0]
    n_tiles = n_tok // tm
    tps = seq_len // tm
    assert tm % WINDOW == 0 and tm % ML_CHUNK == 0 and seq_len % tm == 0
    mix = lambda t: jnp.minimum(t, n_tiles - 1)
    post = lambda t: jnp.maximum(t - 1, 0)
    tok = lambda w: pl.BlockSpec((tm, w), lambda t: (post(t), 0))
    tok_mix = lambda w: pl.BlockSpec((tm, w), lambda t: (mix(t), 0))
    seq = lambda w: pl.BlockSpec((None, tm, w), lambda t: (mix(t) // tps, mix(t) % tps, 0))
    seq_t = lambda w: pl.BlockSpec((None, w, tm), lambda t: (mix(t) // tps, 0, mix(t) % tps))
    weights = (wa, wb, wo, fg, fwi, fwo, pg, pwg, pwp)
    return pl.pallas_call(
        functools.partial(_mixpost_body, tiles_per_seq=tps),
        grid=(n_tiles + 1,),
        in_specs=[tok_mix(D_MODEL), tok_mix(2 * D_MODEL),
                  pl.BlockSpec((None, tm, P_DIM), lambda t: (layer, post(t), 0)),
                  seq(TOK_W), seq_t(TR_W), seq_t(4 * ML_HEADS),
                  _const_spec(bias2.shape), _layer_spec(sink, layer), _layer_spec(ml_gain, layer)]
                 + [_layer_spec(w, layer) for w in weights],
        out_specs=tok(D_MODEL),
        out_shape=jax.ShapeDtypeStruct((n_tok, D_MODEL), F32),
        scratch_shapes=[pltpu.VMEM((tm, D_MODEL), BF16), pltpu.VMEM((tm, D_FF), BF16),
                        pltpu.VMEM((tm, D_MODEL), F32),
                        pltpu.VMEM((tm, ML_V_W), BF16), pltpu.VMEM((tm, SW_Q_W), BF16),
                        pltpu.VMEM((ML_HEADS, ML_QK_DIM, 2 * ML_V_DIM), F32),
                        pltpu.VMEM((2 * ML_HEADS, V7X_LANES), F32),
                        pltpu.VMEM((SW_KV_W, WINDOW), BF16), pltpu.VMEM((WINDOW, SW_KV_W), BF16)],
        compiler_params=_params("arbitrary"),
        name="mixpost",
    )(x2, gate, p3, tok_major, transposed, gate_rows, bias2, sink, ml_gain, *weights)


def _t5_bucket(dist):
    max_exact = REL_BUCKETS // 2
    d = np.maximum(dist, 0)
    large = max_exact + (np.log(np.maximum(d, 1) / max_exact) / np.log(REL_MAX_DIST / max_exact)
                         * (REL_BUCKETS - max_exact)).astype(np.int32)
    large = np.minimum(large, REL_BUCKETS - 1)
    return np.where(d < max_exact, d, large).astype(np.int32)


def _swa_bias(rel_bias):
    W = WINDOW
    pairs = SW_GROUP // 2
    dist = np.arange(W)[:, None] + W - np.arange(2 * W)[None, :]
    in_window = (dist >= 0) & (dist < W)
    onehot = (_t5_bucket(dist).reshape(-1)[None, :] == np.arange(REL_BUCKETS)[:, None]).astype(np.float32)
    bias = jnp.dot(rel_bias.astype(F32).T, onehot, precision=jax.lax.Precision.HIGHEST)
    bias = bias.reshape(SW_KV_HEADS, pairs, 2, W, 2 * W)
    bias = jnp.transpose(bias, (0, 2, 1, 3, 4)).reshape(SW_KV_HEADS, 2, pairs * W, 2 * W)
    mask_later = np.tile(in_window, (pairs, 1))
    mask_first = np.tile(in_window & (np.arange(2 * W) >= W)[None, :], (pairs, 1))
    return jnp.stack([jnp.where(mask_first, bias, -jnp.inf), jnp.where(mask_later, bias, -jnp.inf)])


def kernel(x, p, ffn1_norm, ffn1_wi, ffn1_wo, mix_norm, w_in, b_igate, b_fgate, ml_out_norm, q_norm, k_norm, sinks, rel_bias, w_a, w_b, w_out, ffn2_norm, ffn2_wi, ffn2_wo, ple_norm, w_ple_gate, w_ple):
    B, S, _ = x.shape
    depth = p.shape[0]
    n_tok = B * S
    tm = min(TOKEN_TILE, S)
    offs = [0] + [int(o) for o in np.cumsum(SPLITS)]

    bf = lambda w: w.astype(BF16)
    rows = lambda v: v.reshape(depth, 1, -1).astype(F32)
    w_in_b = bf(w_in)
    col = lambda a, b: w_in_b[:, :, offs[a]:offs[b]]
    wm = jnp.concatenate([col(0, 1), col(2, 4)], axis=2)
    wt = jnp.swapaxes(jnp.concatenate([col(1, 2), col(4, 6)], axis=2), 1, 2)
    wt = jnp.pad(wt, ((0, 0), (0, -wt.shape[1] % V7X_BF16_SUBLANES), (0, 0)))
    ws = col(6, 9)
    wg = col(9, 11)
    qk_gain = jnp.concatenate([jnp.tile(q_norm, (1, SW_Q_HEADS)) * (SW_HEAD_DIM ** -0.5),
                               jnp.tile(k_norm, (1, SW_KV_HEADS))], axis=1).reshape(depth, 1, -1).astype(F32)
    gate_bias = jnp.concatenate([b_igate, b_fgate], axis=1).reshape(depth, -1, 1).astype(F32)
    sink = jnp.broadcast_to(sinks.astype(F32)[:, :, None], (depth, SW_Q_HEADS, V7X_LANES))
    ml_gain = ml_out_norm.astype(F32)
    f1g, f1wi, f1wo = rows(ffn1_norm), bf(ffn1_wi), bf(ffn1_wo)
    f2g, f2wi, f2wo = rows(ffn2_norm), bf(ffn2_wi), bf(ffn2_wo)
    mixg, pleg = rows(mix_norm), rows(ple_norm)
    wa, wb, wo, wpg, wpp = bf(w_a), bf(w_b), bf(w_out), bf(w_ple_gate), bf(w_ple)
    p3 = p.reshape(depth, n_tok, P_DIM)

    bias2 = _swa_bias(rel_bias)
    head_of = np.arange(V7X_MXU_DIM) // SW_HEAD_DIM
    group_ones = jnp.asarray(head_of[:, None] == head_of[None, :], BF16)

    flat = lambda a: a.reshape(n_tok, a.shape[-1])
    x2 = x.reshape(n_tok, D_MODEL)
    for i in range(depth):
        x2 = _ffn(x2, f1g, f1wi, f1wo, i, min(FFN_TOKEN_TILE, n_tok))
        tok_major, gate, transposed, gate_rows = _inproj(
            x2.reshape(B, S, D_MODEL), mixg, wm, wt, gate_bias, ws, wg, qk_gain, group_ones, i,
            min(INPROJ_TOKEN_TILE, S))
        x2 = _mixpost(x2, flat(gate), p3, tok_major, transposed, gate_rows, bias2, sink, ml_gain,
                      wa, wb, wo, f2g, f2wi, f2wo, pleg, wpg, wpp, i, S, tm)
    return x2.reshape(B, S, D_MODEL)
```

```python
import functools
import math

import jax
import jax.numpy as jnp
import numpy as np
from jax.experimental import pallas as pl
from jax.experimental.pallas import tpu as pltpu

D_MODEL = 1024
P_DIM = 256
D_FF = 2816
ML_HEADS = 4
ML_QK_DIM = 64
ML_V_DIM = 128
SW_Q_HEADS = 8
SW_KV_HEADS = 2
SW_HEAD_DIM = 64
WINDOW = 128
REL_BUCKETS = 32
REL_MAX_DIST = 128
EPS = 1e-6

ML_QK_W = ML_HEADS * ML_QK_DIM
ML_V_W = ML_HEADS * ML_V_DIM
SW_Q_W = SW_Q_HEADS * SW_HEAD_DIM
SW_KV_W = SW_KV_HEADS * SW_HEAD_DIM
SW_GROUP = SW_Q_HEADS // SW_KV_HEADS
SPLITS = (ML_QK_W, ML_QK_W, ML_V_W, ML_V_W, ML_HEADS, ML_HEADS, SW_Q_W, SW_KV_W, SW_KV_W, D_MODEL, D_MODEL)

TOK_MV = ML_QK_W
TOK_MO = TOK_MV + ML_V_W
TOK_SQ = TOK_MO + ML_V_W
TOK_SV = TOK_SQ + SW_Q_W
TOK_W = TOK_SV + SW_KV_W
TR_W = ML_QK_W + SW_KV_W

V7X_LANES = 128
V7X_MXU_DIM = 256
V7X_BF16_SUBLANES = 16
V7X_VMEM_BYTES = 64 * 1024 * 1024
V7X_VMEM_LIMIT_BYTES = V7X_VMEM_BYTES - 4 * 1024 * 1024

FFN_CHUNK = 256
ML_CHUNK = 128
TOKEN_TILE = 512
FFN_TOKEN_TILE = 1024
INPROJ_TOKEN_TILE = 1024

F32 = jnp.float32
BF16 = jnp.bfloat16


def _dot(a, b):
    return jnp.dot(a, b, preferred_element_type=F32)


def lax_rsqrt(v):
    return jax.lax.rsqrt(v)


def _rms_scale(xf):
    return lax_rsqrt(jnp.mean(xf * xf, axis=-1, keepdims=True) + EPS)


def _sigmoid(v):
    return 1.0 / (1.0 + jnp.exp(-v))


def _const_spec(shape):
    nd = len(shape)
    return pl.BlockSpec(shape, lambda *_: (0,) * nd, pipeline_mode=pl.Buffered(1))


def _layer_spec(stacked, layer):
    tail = stacked.shape[1:]
    return pl.BlockSpec((None,) + tail, lambda *_: (layer,) + (0,) * len(tail), pipeline_mode=pl.Buffered(1))


def _params(*sem):
    return pltpu.CompilerParams(dimension_semantics=sem, vmem_limit_bytes=V7X_VMEM_LIMIT_BYTES)


def _ffn_half_step(xf, g_ref, wi_ref, wo_ref, n_ref, act_ref):
    n_ref[...] = (xf * _rms_scale(xf) * g_ref[...]).astype(BF16)
    for j in range(D_FF // FFN_CHUNK):
        lo = j * FFN_CHUNK
        hg = _dot(n_ref[...], wi_ref[:, lo:lo + FFN_CHUNK])
        hu = _dot(n_ref[...], wi_ref[:, D_FF + lo:D_FF + lo + FFN_CHUNK])
        act_ref[:, lo:lo + FFN_CHUNK] = (hg * _sigmoid(hg) * hu).astype(BF16)
    return xf + 0.5 * _dot(act_ref[...], wo_ref[...])


def _ffn_body(x_ref, g_ref, wi_ref, wo_ref, o_ref, n_ref, act_ref):
    o_ref[...] = _ffn_half_step(x_ref[...], g_ref, wi_ref, wo_ref, n_ref, act_ref)


def _ffn(x2, g, wi, wo, layer, tm):
    n_tok = x2.shape[0]
    return pl.pallas_call(
        _ffn_body,
        grid=(n_tok // tm,),
        in_specs=[pl.BlockSpec((tm, D_MODEL), lambda i: (i, 0)),
                  _layer_spec(g, layer), _layer_spec(wi, layer), _layer_spec(wo, layer)],
        out_specs=pl.BlockSpec((tm, D_MODEL), lambda i: (i, 0)),
        out_shape=jax.ShapeDtypeStruct((n_tok, D_MODEL), F32),
        scratch_shapes=[pltpu.VMEM((tm, D_MODEL), BF16), pltpu.VMEM((tm, D_FF), BF16)],
        compiler_params=_params("parallel"),
        name="ffn",
    )(x2, g, wi, wo)


def _inproj_body(x_ref, g_ref, wm_ref, wt_ref, bif_ref, ws_ref, wg_ref, gain_ref, ones_ref,
                 tok_ref, gate_ref, tr_ref, gt_ref):
    xf = x_ref[...]
    u = (xf * _rms_scale(xf) * g_ref[...]).astype(BF16)
    tok_ref[:, :TOK_SQ] = _dot(u, wm_ref[...]).astype(BF16)
    nt = (((1,), (1,)), ((), ()))
    zt = jax.lax.dot_general(wt_ref[...], u, nt, preferred_element_type=F32)
    tr_ref[:ML_QK_W, :] = (zt[:ML_QK_W] * (1.0 / math.sqrt(ML_QK_DIM))).astype(BF16)
    pre = zt[ML_QK_W:ML_QK_W + 2 * ML_HEADS] + bif_ref[...]
    logf = jnp.minimum(pre, 0.0) - jnp.log1p(jnp.exp(-jnp.abs(pre)))
    pos = jax.lax.broadcasted_iota(jnp.int32, pre.shape, 1) & (ML_CHUNK - 1)
    csum = logf
    k = 1
    while k < ML_CHUNK:
        csum = csum + jnp.where(pos >= k, pltpu.roll(csum, k, 1), 0.0)
        k *= 2
    gt_ref[:2 * ML_HEADS, :] = jnp.where(jax.lax.broadcasted_iota(jnp.int32, pre.shape, 0) < ML_HEADS, pre, logf)
    gt_ref[2 * ML_HEADS:, :] = csum
    zs = _dot(u, ws_ref[...])
    qk = zs[:, :SW_Q_W + SW_KV_W]
    sq2 = (qk * qk).astype(BF16)
    slab = ones_ref.shape[0]
    widths = [min(slab, sq2.shape[1] - lo) for lo in range(0, sq2.shape[1], slab)]
    ssq = jnp.concatenate([_dot(sq2[:, i * slab:i * slab + w], ones_ref[:w, :w]) for i, w in enumerate(widths)],
                          axis=1)
    qkn = qk * lax_rsqrt(ssq * (1.0 / SW_HEAD_DIM) + EPS) * gain_ref[...]
    tok_ref[:, TOK_SQ:TOK_SV] = qkn[:, :SW_Q_W].astype(BF16)
    tr_ref[ML_QK_W:, :] = qkn[:, SW_Q_W:].T.astype(BF16)
    tok_ref[:, TOK_SV:] = zs[:, SW_Q_W + SW_KV_W:].astype(BF16)
    gate_ref[...] = _dot(u, wg_ref[...]).astype(BF16)


def _inproj(x3, g, wm, wt, bif, ws, wg, gain, ones, layer, tm):
    b, s, _ = x3.shape
    assert tm % ML_CHUNK == 0
    tok = lambda w: pl.BlockSpec((None, tm, w), lambda i, j: (i, j, 0))
    tok_t = lambda w: pl.BlockSpec((None, w, tm), lambda i, j: (i, 0, j))
    sd = jax.ShapeDtypeStruct
    return pl.pallas_call(
        _inproj_body,
        grid=(b, s // tm),
        in_specs=[tok(D_MODEL)] + [_layer_spec(a, layer) for a in (g, wm, wt, bif, ws, wg, gain)]
                 + [_const_spec(ones.shape)],
        out_specs=[tok(TOK_W), tok(2 * D_MODEL), tok_t(TR_W), tok_t(4 * ML_HEADS)],
        out_shape=[sd((b, s, TOK_W), BF16), sd((b, s, 2 * D_MODEL), BF16), sd((b, TR_W, s), BF16),
                   sd((b, 4 * ML_HEADS, s), F32)],
        compiler_params=_params("parallel", "parallel"),
        name="inproj",
    )(x3, g, wm, wt, bif, ws, wg, gain, ones)


def _mlstm_pieces(q_ref, kt_ref, v_ref, o_ref, gl_ref, gc_ref, gain_ref, h_ref, c_ref, m_ref):
    L = ML_CHUNK
    H, dk, dv = ML_HEADS, ML_QK_DIM, ML_V_DIM
    heads = range(H)
    causal = (jax.lax.broadcasted_iota(jnp.int32, (L, L), 1)
              <= jax.lax.broadcasted_iota(jnp.int32, (L, L), 0))
    ones_col = (jax.lax.broadcasted_iota(jnp.int32, (L, V7X_LANES), 1) == 0).astype(BF16)
    q_head = jax.lax.broadcasted_iota(jnp.int32, (L, H * dk), 1) // dk
    zeros_c = jnp.zeros((dk, 2 * dv), BF16)
    zeros_k = jnp.zeros((dk, 2 * dk), BF16)

    def chunk(c):
        sl = slice(c * L, (c + 1) * L)
        st = {}

        def gates():
            gl = gl_ref[:, sl]
            gc = gc_ref[:, sl]
            b_rows = [gc[H + h:H + h + 1, :] for h in heads]
            c_rows = [gl[h:h + 1, :] - b_rows[h] for h in heads]
            st["m_prev"] = [m_ref[h:h + 1, 0:1] for h in heads]
            st["g_tot"] = [b_rows[h][:, L - 1:L] for h in heads]
            st["a_row"] = [st["g_tot"][h] + c_rows[h] for h in heads]
            st["m_new"] = [jnp.maximum(st["g_tot"][h] + st["m_prev"][h],
                                       jnp.max(st["a_row"][h], axis=-1, keepdims=True)) for h in heads]
            st["c_rows"] = c_rows
            st["f_rows"] = [gl[H + h:H + h + 1, :] for h in heads]

        def scores():
            q = q_ref[sl, :]
            q_sel = jnp.concatenate([jnp.where(q_head == h, q, jnp.zeros((), BF16)) for h in heads], axis=0)
            st["s_all"] = _dot(q_sel, kt_ref[:, sl])

        def weights():
            st["m_col"], st["b_col"], st["w"] = [], [], []
            for h in heads:
                dm = jnp.where(causal, st["c_rows"][h], -jnp.inf)
                m_col = jnp.maximum(jnp.max(dm, axis=-1, keepdims=True), st["m_prev"][h])
                st["m_col"].append(m_col)
                st["b_col"].append(jnp.sum(jnp.where(causal, st["f_rows"][h], 0.0), axis=-1, keepdims=True))
                st["w"].append((jnp.exp(dm - m_col) * st["s_all"][h * L:(h + 1) * L, :]).astype(BF16))
            st["res"] = [None] * H

        def head_matmul(h):
            def run():
                pair = h // 2
                v_aug = jnp.concatenate([v_ref[sl, h * dv:(h + 1) * dv], ones_col], axis=1)
                st.setdefault("c_aug", {})[h] = c_ref[h]
                c_bf = st["c_aug"][h].astype(BF16)
                rhs = jnp.concatenate([v_aug] + ([c_bf, zeros_c] if h % 2 == 0 else [zeros_c, c_bf]), axis=0)
                s_inter = jnp.exp(st["m_prev"][h] - st["m_col"][h])
                q_s = (q_ref[sl, pair * 2 * dk:(pair + 1) * 2 * dk].astype(F32) * s_inter).astype(BF16)
                kw = (kt_ref[h * dk:(h + 1) * dk, sl].astype(F32)
                      * jnp.exp(st["a_row"][h] - st["m_new"][h])).astype(BF16)
                lhs = jnp.concatenate([jnp.concatenate([st["w"][h], q_s], axis=1),
                                       jnp.concatenate([kw, zeros_k], axis=1)], axis=0)
                st["res"][h] = _dot(lhs, rhs)
            return run

        def update():
            for h in heads:
                decay = jnp.exp(st["g_tot"][h] + st["m_prev"][h] - st["m_new"][h])
                c_ref[h] = decay * st["c_aug"][h] + st["res"][h][L:, :]
                m_ref[h:h + 1, :] = jnp.broadcast_to(st["m_new"][h], (1, V7X_LANES))

        def output(h):
            def run():
                res = st["res"][h]
                inv = 1.0 / jnp.maximum(jnp.abs(res[:L, dv:dv + 1]), jnp.exp(-(st["b_col"][h] + st["m_col"][h])))
                hh = res[:L, :dv] * inv
                hn = hh * lax_rsqrt(jnp.mean(hh * hh, axis=-1, keepdims=True) + EPS) * gain_ref[h:h + 1, :]
                gate = _sigmoid(o_ref[sl, h * dv:(h + 1) * dv].astype(F32))
                h_ref[sl, h * dv:(h + 1) * dv] = (gate * hn).astype(BF16)
            return run

        return ([gates, scores, weights] + [head_matmul(h) for h in heads] + [update]
                + [output(h) for h in heads])

    pieces = []
    for c in range(q_ref.shape[0] // L):
        pieces += chunk(c)
    return pieces


def _swa_pieces(q_ref, kt_ref, ktp_ref, v_ref, vp_ref, bias_ref, first_sel, sink_ref, o_ref):
    W, d = WINDOW, SW_HEAD_DIM
    pairs = SW_GROUP // 2
    rows = pairs * W
    groups = range(SW_KV_HEADS)
    lane = jax.lax.broadcasted_iota(jnp.int32, (rows, 2 * d), 1)
    zeros_kt = jnp.zeros((d, 2 * W), BF16)
    ones_v = jnp.ones((2 * W, 2 * d), BF16)

    def block(blk):
        cur = slice(blk * W, (blk + 1) * W)
        prev = slice((blk - 1) * W, blk * W)
        st = {}

        def scores():
            st["scores"], st["vv"] = [], []
            for g in groups:
                hd = slice(g * d, (g + 1) * d)
                kt_prev = ktp_ref[hd, :] if blk == 0 else kt_ref[hd, prev]
                v_prev = vp_ref[:, hd] if blk == 0 else v_ref[prev, hd]
                kt = jnp.concatenate([kt_prev, kt_ref[hd, cur]], axis=1)
                vg = jnp.concatenate([v_prev, v_ref[cur, hd]], axis=0)
                st["vv"].append(jnp.concatenate([vg, vg, ones_v], axis=1))
                kt_sel = jnp.concatenate([jnp.concatenate([kt, zeros_kt], axis=0),
                                          jnp.concatenate([zeros_kt, kt], axis=0)], axis=1)
                q_rows = jnp.concatenate([q_ref[cur, (g * pairs + pr) * 2 * d:(g * pairs + pr + 1) * 2 * d]
                                          for pr in range(pairs)], axis=0)
                st["scores"].append(_dot(q_rows, kt_sel))

        def softmax():
            st["probs"], st["tail"] = [], []
            for g in groups:
                for sub in range(2):
                    bias = bias_ref[first_sel, g, sub] if blk == 0 else bias_ref[1, g, sub]
                    logits = st["scores"][g][:, sub * 2 * W:(sub + 1) * 2 * W] + bias
                    sink = jnp.concatenate(
                        [jnp.broadcast_to(sink_ref[g * SW_GROUP + 2 * pr + sub:g * SW_GROUP + 2 * pr + sub + 1, :],
                                          (W, V7X_LANES)) for pr in range(pairs)], axis=0)
                    m = jnp.maximum(jnp.broadcast_to(jnp.max(logits, axis=-1, keepdims=True), (rows, V7X_LANES)),
                                    sink)
                    st["probs"].append(jnp.exp(logits - jnp.concatenate([m, m], axis=1)).astype(BF16))
                    st["tail"].append(jnp.exp(sink - m))

        def values():
            res = [_dot(st["probs"][2 * g + sub], st["vv"][g]) for g in groups for sub in range(2)]
            for g in groups:
                outs = [res[2 * g + sub][:, :2 * d] / (res[2 * g + sub][:, 2 * d:] + st["tail"][2 * g + sub])
                        for sub in range(2)]
                sel = jnp.where(lane < d, outs[0], outs[1]).astype(BF16)
                for pr in range(pairs):
                    o_ref[cur, (g * pairs + pr) * 2 * d:(g * pairs + pr + 1) * 2 * d] = sel[pr * W:(pr + 1) * W]

        return [scores, softmax, values]

    pieces = []
    for blk in range(q_ref.shape[0] // W):
        pieces += block(blk)
    return pieces


def _merge_lists(a, b):
    out, ia, ib = [], 0, 0
    while ia < len(a) or ib < len(b):
        if ib >= len(b) or (ia < len(a) and ia * len(b) <= ib * len(a)):
            out.append(a[ia]); ia += 1
        else:
            out.append(b[ib]); ib += 1
    return out


def _mixpost_body(x_ref, gate_ref, p_ref, tok_ref, tr_ref, gt_ref,
                  bias_ref, sink_ref, mlg_ref, wa_ref, wb_ref, wo_ref, fg_ref, fwi_ref, fwo_ref,
                  pg_ref, pwg_ref, pwp_ref, o_ref,
                  n_ref, act_ref, x1_ref, ha_ref, hb_ref, c_ref, m_ref, ktp_ref, vp_ref, *, tiles_per_seq):
    t = pl.program_id(0)
    n_tiles = pl.num_programs(0) - 1
    seq_start = jnp.minimum(t, n_tiles - 1) % tiles_per_seq == 0

    @pl.when(seq_start)
    def _():
        c_ref[...] = jnp.zeros_like(c_ref)
        m_ref[...] = jnp.zeros_like(m_ref)
        ktp_ref[...] = jnp.zeros_like(ktp_ref)
        vp_ref[...] = jnp.zeros_like(vp_ref)

    first_sel = jnp.where(seq_start, 0, 1)
    cols = lambda lo, hi: tok_ref.at[:, lo:hi]
    mq_ref, mv_ref, mo_ref = cols(0, TOK_MV), cols(TOK_MV, TOK_MO), cols(TOK_MO, TOK_SQ)
    sq_ref, sv_ref = cols(TOK_SQ, TOK_SV), cols(TOK_SV, TOK_W)
    mkt_ref, skt_ref = tr_ref.at[:ML_QK_W, :], tr_ref.at[ML_QK_W:, :]
    gl_ref, gc_ref = gt_ref.at[:2 * ML_HEADS, :], gt_ref.at[2 * ML_HEADS:, :]

    def keep_last_block():
        ktp_ref[...] = skt_ref[:, skt_ref.shape[1] - WINDOW:]
        vp_ref[...] = sv_ref[sv_ref.shape[0] - WINDOW:, :]

    def mixer_pieces():
        return _merge_lists(
            _mlstm_pieces(mq_ref, mkt_ref, mv_ref, mo_ref, gl_ref, gc_ref, mlg_ref, ha_ref, c_ref, m_ref),
            _swa_pieces(sq_ref, skt_ref, ktp_ref, sv_ref, vp_ref, bias_ref, first_sel, sink_ref, hb_ref)
        ) + [keep_last_block]

    def merge_gated():
        ya = _dot(ha_ref[...], wa_ref[...])
        yb = _dot(hb_ref[...], wb_ref[...])
        return (_sigmoid(gate_ref[:, :D_MODEL].astype(F32)) * ya
                + _sigmoid(gate_ref[:, D_MODEL:].astype(F32)) * yb).astype(BF16)

    def carry(x1):
        x1_ref[...] = x1
        n_ref[...] = (x1 * _rms_scale(x1) * fg_ref[...]).astype(BF16)

    @pl.when(t == 0)
    def _():
        for piece in mixer_pieces():
            piece()
        carry(x_ref[...] + _dot(merge_gated(), wo_ref[...]))

    @pl.when(t > 0)
    def _():
        pieces = mixer_pieces()
        n_chunks = D_FF // FFN_CHUNK
        for j in range(n_chunks):
            lo = j * FFN_CHUNK
            hg = _dot(n_ref[...], fwi_ref[:, lo:lo + FFN_CHUNK])
            hu = _dot(n_ref[...], fwi_ref[:, D_FF + lo:D_FF + lo + FFN_CHUNK])
            act_ref[:, lo:lo + FFN_CHUNK] = (hg * _sigmoid(hg) * hu).astype(BF16)
            for piece in pieces[j * len(pieces) // n_chunks:(j + 1) * len(pieces) // n_chunks]:
                piece()
        x2 = x1_ref[...] + 0.5 * _dot(act_ref[...], fwo_ref[...])
        mixed = merge_gated()
        n_ple = (x2 * _rms_scale(x2) * pg_ref[...]).astype(BF16)
        x1 = x_ref[...] + _dot(mixed, wo_ref[...])
        gate = _sigmoid(_dot(n_ple, pwg_ref[...]))
        o_ref[...] = x2 + gate * _dot(p_ref[...].astype(BF16), pwp_ref[...])
        carry(x1)


def _mixpost(x2, gate, p3, tok_major, transposed, gate_rows, bias2, sink, ml_gain,
             wa, wb, wo, fg, fwi, fwo, pg, pwg, pwp, layer, seq_len, tm):
    n_tok = x2.shape[0]
    n_tiles = n_tok // tm
    tps = seq_len // tm
    assert tm % WINDOW == 0 and tm % ML_CHUNK == 0 and seq_len % tm == 0
    mix = lambda t: jnp.minimum(t, n_tiles - 1)
    post = lambda t: jnp.maximum(t - 1, 0)
    tok = lambda w: pl.BlockSpec((tm, w), lambda t: (post(t), 0))
    tok_mix = lambda w: pl.BlockSpec((tm, w), lambda t: (mix(t), 0))
    seq = lambda w: pl.BlockSpec((None, tm, w), lambda t: (mix(t) // tps, mix(t) % tps, 0))
    seq_t = lambda w: pl.BlockSpec((None, w, tm), lambda t: (mix(t) // tps, 0, mix(t) % tps))
    weights = (wa, wb, wo, fg, fwi, fwo, pg, pwg, pwp)
    return pl.pallas_call(
        functools.partial(_mixpost_body, tiles_per_seq=tps),
        grid=(n_tiles + 1,),
        in_specs=[tok_mix(D_MODEL), tok_mix(2 * D_MODEL),
                  pl.BlockSpec((None, tm, P_DIM), lambda t: (layer, post(t), 0)),
                  seq(TOK_W), seq_t(TR_W), seq_t(4 * ML_HEADS),
                  _const_spec(bias2.shape), _layer_spec(sink, layer), _layer_spec(ml_gain, layer)]
                 + [_layer_spec(w, layer) for w in weights],
        out_specs=tok(D_MODEL),
        out_shape=jax.ShapeDtypeStruct((n_tok, D_MODEL), F32),
        scratch_shapes=[pltpu.VMEM((tm, D_MODEL), BF16), pltpu.VMEM((tm, D_FF), BF16),
                        pltpu.VMEM((tm, D_MODEL), F32),
                        pltpu.VMEM((tm, ML_V_W), BF16), pltpu.VMEM((tm, SW_Q_W), BF16),
                        pltpu.VMEM((ML_HEADS, ML_QK_DIM, 2 * ML_V_DIM), F32),
                        pltpu.VMEM((2 * ML_HEADS, V7X_LANES), F32),
                        pltpu.VMEM((SW_KV_W, WINDOW), BF16), pltpu.VMEM((WINDOW, SW_KV_W), BF16)],
        compiler_params=_params("arbitrary"),
        name="mixpost",
    )(x2, gate, p3, tok_major, transposed, gate_rows, bias2, sink, ml_gain, *weights)


def _t5_bucket(dist):
    max_exact = REL_BUCKETS // 2
    d = np.maximum(dist, 0)
    large = max_exact + (np.log(np.maximum(d, 1) / max_exact) / np.log(REL_MAX_DIST / max_exact)
                         * (REL_BUCKETS - max_exact)).astype(np.int32)
    large = np.minimum(large, REL_BUCKETS - 1)
    return np.where(d < max_exact, d, large).astype(np.int32)


def _swa_bias(rel_bias):
    W = WINDOW
    pairs = SW_GROUP // 2
    dist = np.arange(W)[:, None] + W - np.arange(2 * W)[None, :]
    in_window = (dist >= 0) & (dist < W)
    onehot = (_t5_bucket(dist).reshape(-1)[None, :] == np.arange(REL_BUCKETS)[:, None]).astype(np.float32)
    bias = jnp.dot(rel_bias.astype(F32).T, onehot, precision=jax.lax.Precision.HIGHEST)
    bias = bias.reshape(SW_KV_HEADS, pairs, 2, W, 2 * W)
    bias = jnp.transpose(bias, (0, 2, 1, 3, 4)).reshape(SW_KV_HEADS, 2, pairs * W, 2 * W)
    mask_later = np.tile(in_window, (pairs, 1))
    mask_first = np.tile(in_window & (np.arange(2 * W) >= W)[None, :], (pairs, 1))
    return jnp.stack([jnp.where(mask_first, bias, -jnp.inf), jnp.where(mask_later, bias, -jnp.inf)])


def kernel(x, p, ffn1_norm, ffn1_wi, ffn1_wo, mix_norm, w_in, b_igate, b_fgate, ml_out_norm, q_norm, k_norm, sinks, rel_bias, w_a, w_b, w_out, ffn2_norm, ffn2_wi, ffn2_wo, ple_norm, w_ple_gate, w_ple):
    B, S, _ = x.shape
    depth = p.shape[0]
    n_tok = B * S
    tm = min(TOKEN_TILE, S)
    offs = [0] + [int(o) for o in np.cumsum(SPLITS)]

    bf = lambda w: w.astype(BF16)
    rows = lambda v: v.reshape(depth, 1, -1).astype(F32)
    w_in_b = bf(w_in)
    col = lambda a, b: w_in_b[:, :, offs[a]:offs[b]]
    wm = jnp.concatenate([col(0, 1), col(2, 4)], axis=2)
    wt = jnp.swapaxes(jnp.concatenate([col(1, 2), col(4, 6)], axis=2), 1, 2)
    wt = jnp.pad(wt, ((0, 0), (0, -wt.shape[1] % V7X_BF16_SUBLANES), (0, 0)))
    ws = col(6, 9)
    wg = col(9, 11)
    qk_gain = jnp.concatenate([jnp.tile(q_norm, (1, SW_Q_HEADS)) * (SW_HEAD_DIM ** -0.5),
                               jnp.tile(k_norm, (1, SW_KV_HEADS))], axis=1).reshape(depth, 1, -1).astype(F32)
    gate_bias = jnp.concatenate([b_igate, b_fgate], axis=1).reshape(depth, -1, 1).astype(F32)
    sink = jnp.broadcast_to(sinks.astype(F32)[:, :, None], (depth, SW_Q_HEADS, V7X_LANES))
    ml_gain = ml_out_norm.astype(F32)
    f1g, f1wi, f1wo = rows(ffn1_norm), bf(ffn1_wi), bf(ffn1_wo)
    f2g, f2wi, f2wo = rows(ffn2_norm), bf(ffn2_wi), bf(ffn2_wo)
    mixg, pleg = rows(mix_norm), rows(ple_norm)
    wa, wb, wo, wpg, wpp = bf(w_a), bf(w_b), bf(w_out), bf(w_ple_gate), bf(w_ple)
    p3 = p.reshape(depth, n_tok, P_DIM)

    bias2 = _swa_bias(rel_bias)
    head_of = np.arange(V7X_MXU_DIM) // SW_HEAD_DIM
    group_ones = jnp.asarray(head_of[:, None] == head_of[None, :], BF16)

    flat = lambda a: a.reshape(n_tok, a.shape[-1])
    x2 = x.reshape(n_tok, D_MODEL)
    for i in range(depth):
        x2 = _ffn(x2, f1g, f1wi, f1wo, i, min(FFN_TOKEN_TILE, n_tok))
        tok_major, gate, transposed, gate_rows = _inproj(
            x2.reshape(B, S, D_MODEL), mixg, wm, wt, gate_bias, ws, wg, qk_gain, group_ones, i,
            min(INPROJ_TOKEN_TILE, S))
        x2 = _mixpost(x2, flat(gate), p3, tok_major, transposed, gate_rows, bias2, sink, ml_gain,
                      wa, wb, wo, f2g, f2wi, f2wo, pleg, wpg, wpp, i, S, tm)
    return x2.reshape(B, S, D_MODEL)
```

```python
import functools
import math

import jax
import jax.numpy as jnp
import numpy as np
from jax.experimental import pallas as pl
from jax.experimental.pallas import tpu as pltpu

D_MODEL = 1024
P_DIM = 256
D_FF = 2816
ML_HEADS = 4
ML_QK_DIM = 64
ML_V_DIM = 128
SW_Q_HEADS = 8
SW_KV_HEADS = 2
SW_HEAD_DIM = 64
WINDOW = 128
REL_BUCKETS = 32
REL_MAX_DIST = 128
EPS = 1e-6

ML_QK_W = ML_HEADS * ML_QK_DIM
ML_V_W = ML_HEADS * ML_V_DIM
SW_Q_W = SW_Q_HEADS * SW_HEAD_DIM
SW_KV_W = SW_KV_HEADS * SW_HEAD_DIM
SW_GROUP = SW_Q_HEADS // SW_KV_HEADS
SPLITS = (ML_QK_W, ML_QK_W, ML_V_W, ML_V_W, ML_HEADS, ML_HEADS, SW_Q_W, SW_KV_W, SW_KV_W, D_MODEL, D_MODEL)

TOK_MV = ML_QK_W
TOK_MO = TOK_MV + ML_V_W
TOK_SQ = TOK_MO + ML_V_W
TOK_SV = TOK_SQ + SW_Q_W
TOK_W = TOK_SV + SW_KV_W
TR_W = ML_QK_W + SW_KV_W

V7X_LANES = 128
V7X_MXU_DIM = 256
V7X_BF16_SUBLANES = 16
V7X_VMEM_BYTES = 64 * 1024 * 1024
V7X_VMEM_LIMIT_BYTES = V7X_VMEM_BYTES - 4 * 1024 * 1024

FFN_CHUNK = 256
ML_CHUNK = 128
TOKEN_TILE = 512
FFN_TOKEN_TILE = 1024
INPROJ_TOKEN_TILE = 1024

F32 = jnp.float32
BF16 = jnp.bfloat16


def _dot(a, b):
    return jnp.dot(a, b, preferred_element_type=F32)


def lax_rsqrt(v):
    return jax.lax.rsqrt(v)


def _rms_scale(xf):
    return lax_rsqrt(jnp.mean(xf * xf, axis=-1, keepdims=True) + EPS)


def _sigmoid(v):
    return 1.0 / (1.0 + jnp.exp(-v))


def _const_spec(shape):
    nd = len(shape)
    return pl.BlockSpec(shape, lambda *_: (0,) * nd, pipeline_mode=pl.Buffered(1))


def _layer_spec(stacked, layer):
    tail = stacked.shape[1:]
    return pl.BlockSpec((None,) + tail, lambda *_: (layer,) + (0,) * len(tail), pipeline_mode=pl.Buffered(1))


def _params(*sem):
    return pltpu.CompilerParams(dimension_semantics=sem, vmem_limit_bytes=V7X_VMEM_LIMIT_BYTES)


def _ffn_body(x_ref, g_ref, wi_ref, wo_ref, *rest, n_cast):
    src, o_ref, dst = rest[:n_cast], rest[n_cast], rest[n_cast + 1:2 * n_cast + 1]
    n_ref, act_ref = rest[2 * n_cast + 1:]
    xf = x_ref[...]
    n_ref[...] = (xf * _rms_scale(xf) * g_ref[...]).astype(BF16)
    for j in range(D_FF // FFN_CHUNK):
        lo = j * FFN_CHUNK
        hg = _dot(n_ref[...], wi_ref[:, lo:lo + FFN_CHUNK])
        hu = _dot(n_ref[...], wi_ref[:, D_FF + lo:D_FF + lo + FFN_CHUNK])
        act_ref[:, lo:lo + FFN_CHUNK] = (hg * _sigmoid(hg) * hu).astype(BF16)
        if j % 2 == 1 and j // 2 < n_cast:
            dst[j // 2][...] = src[j // 2][...].astype(BF16)
    o_ref[...] = xf + 0.5 * _dot(act_ref[...], wo_ref[...])


def _ffn(x2, g, wi, wo, layer, w_layer, tm, cast_next=()):
    n_tok = x2.shape[0]
    steps = n_tok // tm
    cast_in, cast_out, cast_shape = [], [], []
    for w, slab in cast_next:
        rows = w.shape[1]
        blocks = max(k for k in range(1, steps + 1) if steps % k == 0 and rows % (k * V7X_BF16_SUBLANES) == 0)
        per = steps // blocks
        block = (None, rows // blocks, w.shape[2])
        cast_in.append(pl.BlockSpec(block, lambda i, per=per, slab=slab: (slab, i // per, 0)))
        cast_out.append(pl.BlockSpec(block, lambda i, per=per: (0, i // per, 0)))
        cast_shape.append(jax.ShapeDtypeStruct((1,) + w.shape[1:], BF16))
    tok = pl.BlockSpec((tm, D_MODEL), lambda i: (i, 0))
    outs = pl.pallas_call(
        functools.partial(_ffn_body, n_cast=len(cast_next)),
        grid=(steps,),
        in_specs=[tok, _layer_spec(g, layer), _layer_spec(wi, w_layer), _layer_spec(wo, w_layer)] + cast_in,
        out_specs=[tok] + cast_out,
        out_shape=[jax.ShapeDtypeStruct((n_tok, D_MODEL), F32)] + cast_shape,
        scratch_shapes=[pltpu.VMEM((tm, D_MODEL), BF16), pltpu.VMEM((tm, D_FF), BF16)],
        compiler_params=_params("arbitrary" if cast_next else "parallel"),
        name="ffn",
    )(x2, g, wi, wo, *[w for w, _ in cast_next])
    return outs[0], tuple(outs[1:])


def _inproj_body(x_ref, g_ref, wm_ref, wt_ref, bif_ref, ws_ref, wg_ref, gain_ref, ones_ref,
                 tok_ref, gate_ref, tr_ref, gt_ref):
    xf = x_ref[...]
    u = (xf * _rms_scale(xf) * g_ref[...]).astype(BF16)
    tok_ref[:, :TOK_SQ] = _dot(u, wm_ref[...]).astype(BF16)
    nt = (((1,), (1,)), ((), ()))
    zt = jax.lax.dot_general(wt_ref[...], u, nt, preferred_element_type=F32)
    tr_ref[:ML_QK_W, :] = (zt[:ML_QK_W] * (1.0 / math.sqrt(ML_QK_DIM))).astype(BF16)
    pre = zt[ML_QK_W:ML_QK_W + 2 * ML_HEADS] + bif_ref[...]
    logf = jnp.minimum(pre, 0.0) - jnp.log1p(jnp.exp(-jnp.abs(pre)))
    pos = jax.lax.broadcasted_iota(jnp.int32, pre.shape, 1) & (ML_CHUNK - 1)
    csum = logf
    k = 1
    while k < ML_CHUNK:
        csum = csum + jnp.where(pos >= k, pltpu.roll(csum, k, 1), 0.0)
        k *= 2
    gt_ref[:2 * ML_HEADS, :] = jnp.where(jax.lax.broadcasted_iota(jnp.int32, pre.shape, 0) < ML_HEADS, pre, logf)
    gt_ref[2 * ML_HEADS:, :] = csum
    zs = _dot(u, ws_ref[...])
    qk = zs[:, :SW_Q_W + SW_KV_W]
    sq2 = (qk * qk).astype(BF16)
    slab = ones_ref.shape[0]
    widths = [min(slab, sq2.shape[1] - lo) for lo in range(0, sq2.shape[1], slab)]
    ssq = jnp.concatenate([_dot(sq2[:, i * slab:i * slab + w], ones_ref[:w, :w]) for i, w in enumerate(widths)],
                          axis=1)
    qkn = qk * lax_rsqrt(ssq * (1.0 / SW_HEAD_DIM) + EPS) * gain_ref[...]
    tok_ref[:, TOK_SQ:TOK_SV] = qkn[:, :SW_Q_W].astype(BF16)
    tr_ref[ML_QK_W:, :] = qkn[:, SW_Q_W:].T.astype(BF16)
    tok_ref[:, TOK_SV:] = zs[:, SW_Q_W + SW_KV_W:].astype(BF16)
    gate_ref[...] = _dot(u, wg_ref[...]).astype(BF16)


def _inproj(x3, g, wm, wt, bif, ws, wg, gain, ones, layer, tm):
    b, s, _ = x3.shape
    assert tm % ML_CHUNK == 0
    tok = lambda w: pl.BlockSpec((None, tm, w), lambda i, j: (i, j, 0))
    tok_t = lambda w: pl.BlockSpec((None, w, tm), lambda i, j: (i, 0, j))
    sd = jax.ShapeDtypeStruct
    return pl.pallas_call(
        _inproj_body,
        grid=(b, s // tm),
        in_specs=[tok(D_MODEL)] + [_layer_spec(a, layer) for a in (g, wm, wt, bif, ws, wg, gain)]
                 + [_const_spec(ones.shape)],
        out_specs=[tok(TOK_W), tok(2 * D_MODEL), tok_t(TR_W), tok_t(4 * ML_HEADS)],
        out_shape=[sd((b, s, TOK_W), BF16), sd((b, s, 2 * D_MODEL), BF16), sd((b, TR_W, s), BF16),
                   sd((b, 4 * ML_HEADS, s), F32)],
        compiler_params=_params("parallel", "parallel"),
        name="inproj",
    )(x3, g, wm, wt, bif, ws, wg, gain, ones)


def _mlstm_pieces(q_ref, kt_ref, v_ref, o_ref, gl_ref, gc_ref, gain_ref, h_ref, c_ref, m_ref):
    L = ML_CHUNK
    H, dk, dv = ML_HEADS, ML_QK_DIM, ML_V_DIM
    heads = range(H)
    causal = (jax.lax.broadcasted_iota(jnp.int32, (L, L), 1)
              <= jax.lax.broadcasted_iota(jnp.int32, (L, L), 0))
    ones_col = (jax.lax.broadcasted_iota(jnp.int32, (L, V7X_LANES), 1) == 0).astype(BF16)
    q_head = jax.lax.broadcasted_iota(jnp.int32, (L, H * dk), 1) // dk
    zeros_c = jnp.zeros((dk, 2 * dv), BF16)
    zeros_k = jnp.zeros((dk, 2 * dk), BF16)

    def chunk(c):
        sl = slice(c * L, (c + 1) * L)
        st = {}

        def gates():
            gl = gl_ref[:, sl]
            gc = gc_ref[:, sl]
            b_rows = [gc[H + h:H + h + 1, :] for h in heads]
            c_rows = [gl[h:h + 1, :] - b_rows[h] for h in heads]
            st["m_prev"] = [m_ref[h:h + 1, 0:1] for h in heads]
            st["g_tot"] = [b_rows[h][:, L - 1:L] for h in heads]
            st["a_row"] = [st["g_tot"][h] + c_rows[h] for h in heads]
            st["m_new"] = [jnp.maximum(st["g_tot"][h] + st["m_prev"][h],
                                       jnp.max(st["a_row"][h], axis=-1, keepdims=True)) for h in heads]
            st["c_rows"] = c_rows
            st["f_rows"] = [gl[H + h:H + h + 1, :] for h in heads]
            for h in heads:
                m_ref[h:h + 1, :] = jnp.broadcast_to(st["m_new"][h], (1, V7X_LANES))

        def scores():
            q = q_ref[sl, :]
            q_sel = jnp.concatenate([jnp.where(q_head == h, q, jnp.zeros((), BF16)) for h in heads], axis=0)
            st["s_all"] = _dot(q_sel, kt_ref[:, sl])

        st["m_col"], st["b_col"], st["w"], st["res"] = {}, {}, {}, {}

        def weights(h):
            def run():
                dm = jnp.where(causal, st["c_rows"][h], -jnp.inf)
                m_col = jnp.maximum(jnp.max(dm, axis=-1, keepdims=True), st["m_prev"][h])
                st["m_col"][h] = m_col
                st["b_col"][h] = jnp.sum(jnp.where(causal, st["f_rows"][h], 0.0), axis=-1, keepdims=True)
                st["w"][h] = (jnp.exp(dm - m_col) * st["s_all"][h * L:(h + 1) * L, :]).astype(BF16)
            return run

        def head_matmul(h):
            def run():
                pair = h // 2
                v_aug = jnp.concatenate([v_ref[sl, h * dv:(h + 1) * dv], ones_col], axis=1)
                st.setdefault("c_aug", {})[h] = c_ref[h]
                c_bf = st["c_aug"][h].astype(BF16)
                rhs = jnp.concatenate([v_aug] + ([c_bf, zeros_c] if h % 2 == 0 else [zeros_c, c_bf]), axis=0)
                s_inter = jnp.exp(st["m_prev"][h] - st["m_col"][h])
                q_s = (q_ref[sl, pair * 2 * dk:(pair + 1) * 2 * dk].astype(F32) * s_inter).astype(BF16)
                kw = (kt_ref[h * dk:(h + 1) * dk, sl].astype(F32)
                      * jnp.exp(st["a_row"][h] - st["m_new"][h])).astype(BF16)
                lhs = jnp.concatenate([jnp.concatenate([st["w"][h], q_s], axis=1),
                                       jnp.concatenate([kw, zeros_k], axis=1)], axis=0)
                st["res"][h] = _dot(lhs, rhs)
            return run

        def update(h):
            def run():
                decay = jnp.exp(st["g_tot"][h] + st["m_prev"][h] - st["m_new"][h])
                c_ref[h] = decay * st["c_aug"][h] + st["res"][h][L:, :]
            return run

        def output(h):
            def run():
                res = st["res"][h]
                inv = 1.0 / jnp.maximum(jnp.abs(res[:L, dv:dv + 1]), jnp.exp(-(st["b_col"][h] + st["m_col"][h])))
                hh = res[:L, :dv] * inv
                hn = hh * lax_rsqrt(jnp.mean(hh * hh, axis=-1, keepdims=True) + EPS) * gain_ref[h:h + 1, :]
                gate = _sigmoid(o_ref[sl, h * dv:(h + 1) * dv].astype(F32))
                h_ref[sl, h * dv:(h + 1) * dv] = (gate * hn).astype(BF16)
            return run

        return ([gates, scores] + [f(h) for h in heads for f in (weights, head_matmul)]
                + [f(h) for h in heads for f in (update, output)])

    pieces = []
    for c in range(q_ref.shape[0] // L):
        pieces += chunk(c)
    return pieces


def _swa_pieces(q_ref, kt_ref, ktp_ref, v_ref, vp_ref, bias_ref, first_sel, sink_ref, o_ref):
    W, d = WINDOW, SW_HEAD_DIM
    pairs = SW_GROUP // 2
    rows = pairs * W
    groups = range(SW_KV_HEADS)
    lane = jax.lax.broadcasted_iota(jnp.int32, (rows, 2 * d), 1)
    zero = jnp.zeros((), BF16)
    ones_v = jnp.ones((2 * W, 2 * d), BF16)

    def block(blk):
        cur = slice(blk * W, (blk + 1) * W)
        prev = slice((blk - 1) * W, blk * W)
        st = {}

        st.update(scores={}, vv={}, probs={}, tail={})

        def scores(g):
            def run():
                hd = slice(g * d, (g + 1) * d)
                kt_prev = ktp_ref[hd, :] if blk == 0 else kt_ref[hd, prev]
                v_prev = vp_ref[:, hd] if blk == 0 else v_ref[prev, hd]
                kt = jnp.concatenate([kt_prev, kt_ref[hd, cur]], axis=1)
                vg = jnp.concatenate([v_prev, v_ref[cur, hd]], axis=0)
                st["vv"][g] = jnp.concatenate([vg, vg, ones_v], axis=1)
                q_rows = jnp.concatenate([q_ref[cur, (g * pairs + pr) * 2 * d:(g * pairs + pr + 1) * 2 * d]
                                          for pr in range(pairs)], axis=0)
                q_sel = jnp.concatenate([jnp.where(lane < d, q_rows, zero), jnp.where(lane < d, zero, q_rows)],
                                        axis=0)
                st["scores"][g] = _dot(q_sel, jnp.concatenate([kt, kt], axis=0))
            return run

        def softmax(g, sub):
            def run():
                bias = bias_ref[first_sel, g, sub] if blk == 0 else bias_ref[1, g, sub]
                logits = st["scores"][g][sub * rows:(sub + 1) * rows, :] + bias
                sink = jnp.concatenate(
                    [jnp.broadcast_to(sink_ref[g * SW_GROUP + 2 * pr + sub:g * SW_GROUP + 2 * pr + sub + 1, :],
                                      (W, V7X_LANES)) for pr in range(pairs)], axis=0)
                m = jnp.maximum(jnp.broadcast_to(jnp.max(logits, axis=-1, keepdims=True), (rows, V7X_LANES)),
                                sink)
                st["probs"][g, sub] = jnp.exp(logits - jnp.concatenate([m, m], axis=1)).astype(BF16)
                st["tail"][g, sub] = jnp.exp(sink - m)
            return run

        def values(g):
            def run():
                probs = jnp.concatenate([st["probs"][g, 0], st["probs"][g, 1]], axis=0)
                res = _dot(probs, st["vv"][g])
                outs = [res[sub * rows:(sub + 1) * rows, :2 * d]
                        / (res[sub * rows:(sub + 1) * rows, 2 * d:] + st["tail"][g, sub]) for sub in range(2)]
                sel = jnp.where(lane < d, outs[0], outs[1]).astype(BF16)
                for pr in range(pairs):
                    o_ref[cur, (g * pairs + pr) * 2 * d:(g * pairs + pr + 1) * 2 * d] = sel[pr * W:(pr + 1) * W]
            return run

        return ([scores(g) for g in groups] + [softmax(g, sub) for g in groups for sub in range(2)]
                + [values(g) for g in groups])

    pieces = []
    for blk in range(q_ref.shape[0] // W):
        pieces += block(blk)
    return pieces


def _merge_lists(a, b):
    out, ia, ib = [], 0, 0
    while ia < len(a) or ib < len(b):
        if ib >= len(b) or (ia < len(a) and ia * len(b) <= ib * len(a)):
            out.append(a[ia]); ia += 1
        else:
            out.append(b[ib]); ib += 1
    return out


def _mixpost_body(x_ref, gate_ref, p_ref, tok_ref, tr_ref, gt_ref,
                  bias_ref, sink_ref, mlg_ref, wa_ref, wb_ref, wo_ref, fg_ref, fwi_ref, fwo_ref,
                  pg_ref, pwg_ref, pwp_ref, o_ref,
                  n_ref, act_ref, x1_ref, ha_ref, hb_ref, c_ref, m_ref, ktp_ref, vp_ref, *, tiles_per_seq):
    t = pl.program_id(0)
    n_tiles = pl.num_programs(0) - 1
    seq_start = jnp.minimum(t, n_tiles - 1) % tiles_per_seq == 0

    @pl.when(seq_start)
    def _():
        c_ref[...] = jnp.zeros_like(c_ref)
        m_ref[...] = jnp.zeros_like(m_ref)
        ktp_ref[...] = jnp.zeros_like(ktp_ref)
        vp_ref[...] = jnp.zeros_like(vp_ref)

    first_sel = jnp.where(seq_start, 0, 1)
    cols = lambda lo, hi: tok_ref.at[:, lo:hi]
    mq_ref, mv_ref, mo_ref = cols(0, TOK_MV), cols(TOK_MV, TOK_MO), cols(TOK_MO, TOK_SQ)
    sq_ref, sv_ref = cols(TOK_SQ, TOK_SV), cols(TOK_SV, TOK_W)
    mkt_ref, skt_ref = tr_ref.at[:ML_QK_W, :], tr_ref.at[ML_QK_W:, :]
    gl_ref, gc_ref = gt_ref.at[:2 * ML_HEADS, :], gt_ref.at[2 * ML_HEADS:, :]

    def keep_last_block():
        ktp_ref[...] = skt_ref[:, skt_ref.shape[1] - WINDOW:]
        vp_ref[...] = sv_ref[sv_ref.shape[0] - WINDOW:, :]

    def mixer_pieces():
        return _merge_lists(
            _mlstm_pieces(mq_ref, mkt_ref, mv_ref, mo_ref, gl_ref, gc_ref, mlg_ref, ha_ref, c_ref, m_ref),
            _swa_pieces(sq_ref, skt_ref, ktp_ref, sv_ref, vp_ref, bias_ref, first_sel, sink_ref, hb_ref)
        ) + [keep_last_block]

    def merge_gated():
        ya = _dot(ha_ref[...], wa_ref[...])
        yb = _dot(hb_ref[...], wb_ref[...])
        return (_sigmoid(gate_ref[:, :D_MODEL].astype(F32)) * ya
                + _sigmoid(gate_ref[:, D_MODEL:].astype(F32)) * yb).astype(BF16)

    def carry(x1):
        x1_ref[...] = x1
        n_ref[...] = (x1 * _rms_scale(x1) * fg_ref[...]).astype(BF16)

    @pl.when(t == 0)
    def _():
        for piece in mixer_pieces():
            piece()
        carry(x_ref[...] + _dot(merge_gated(), wo_ref[...]))

    @pl.when(t > 0)
    def _():
        pieces = mixer_pieces()
        n_chunks = D_FF // FFN_CHUNK
        n_slots = 2 * n_chunks

        def run_slot(k):
            for piece in pieces[k * len(pieces) // n_slots:(k + 1) * len(pieces) // n_slots]:
                piece()

        for j in range(n_chunks):
            lo = j * FFN_CHUNK
            hg = _dot(n_ref[...], fwi_ref[:, lo:lo + FFN_CHUNK])
            run_slot(2 * j)
            hu = _dot(n_ref[...], fwi_ref[:, D_FF + lo:D_FF + lo + FFN_CHUNK])
            act_ref[:, lo:lo + FFN_CHUNK] = (hg * _sigmoid(hg) * hu).astype(BF16)
            run_slot(2 * j + 1)
        x2 = x1_ref[...] + 0.5 * _dot(act_ref[...], fwo_ref[...])
        mixed = merge_gated()
        n_ple = (x2 * _rms_scale(x2) * pg_ref[...]).astype(BF16)
        x1 = x_ref[...] + _dot(mixed, wo_ref[...])
        gate = _sigmoid(_dot(n_ple, pwg_ref[...]))
        o_ref[...] = x2 + gate * _dot(p_ref[...].astype(BF16), pwp_ref[...])
        carry(x1)


def _mixpost(x2, gate, p3, tok_major, transposed, gate_rows, bias2, sink, ml_gain,
             wa, wb, wo, fg, fwi, fwo, pg, pwg, pwp, layer, ffn_w_layer, seq_len, tm):
    n_tok = x2.shape[0]
    n_tiles = n_tok // tm
    tps = seq_len // tm
    assert tm % WINDOW == 0 and tm % ML_CHUNK == 0 and seq_len % tm == 0
    mix = lambda t: jnp.minimum(t, n_tiles - 1)
    post = lambda t: jnp.maximum(t - 1, 0)
    tok = lambda w: pl.BlockSpec((tm, w), lambda t: (post(t), 0))
    tok_mix = lambda w: pl.BlockSpec((tm, w), lambda t: (mix(t), 0))
    seq = lambda w: pl.BlockSpec((None, tm, w), lambda t: (mix(t) // tps, mix(t) % tps, 0))
    seq_t = lambda w: pl.BlockSpec((None, w, tm), lambda t: (mix(t) // tps, 0, mix(t) % tps))
    weights = (wa, wb, wo, fg, fwi, fwo, pg, pwg, pwp)
    return pl.pallas_call(
        functools.partial(_mixpost_body, tiles_per_seq=tps),
        grid=(n_tiles + 1,),
        in_specs=[tok_mix(D_MODEL), tok_mix(2 * D_MODEL),
                  pl.BlockSpec((None, tm, P_DIM), lambda t: (layer, post(t), 0)),
                  seq(TOK_W), seq_t(TR_W), seq_t(4 * ML_HEADS),
                  _const_spec(bias2.shape), _layer_spec(sink, layer), _layer_spec(ml_gain, layer)]
                 + [_layer_spec(w, ffn_w_layer if w is fwi or w is fwo else layer) for w in weights],
        out_specs=tok(D_MODEL),
        out_shape=jax.ShapeDtypeStruct((n_tok, D_MODEL), F32),
        scratch_shapes=[pltpu.VMEM((tm, D_MODEL), BF16), pltpu.VMEM((tm, D_FF), BF16),
                        pltpu.VMEM((tm, D_MODEL), F32),
                        pltpu.VMEM((tm, ML_V_W), BF16), pltpu.VMEM((tm, SW_Q_W), BF16),
                        pltpu.VMEM((ML_HEADS, ML_QK_DIM, 2 * ML_V_DIM), F32),
                        pltpu.VMEM((2 * ML_HEADS, V7X_LANES), F32),
                        pltpu.VMEM((SW_KV_W, WINDOW), BF16), pltpu.VMEM((WINDOW, SW_KV_W), BF16)],
        compiler_params=_params("arbitrary"),
        name="mixpost",
    )(x2, gate, p3, tok_major, transposed, gate_rows, bias2, sink, ml_gain, *weights)


def _t5_bucket(dist):
    max_exact = REL_BUCKETS // 2
    d = np.maximum(dist, 0)
    large = max_exact + (np.log(np.maximum(d, 1) / max_exact) / np.log(REL_MAX_DIST / max_exact)
                         * (REL_BUCKETS - max_exact)).astype(np.int32)
    large = np.minimum(large, REL_BUCKETS - 1)
    return np.where(d < max_exact, d, large).astype(np.int32)


def _swa_bias(rel_bias):
    W = WINDOW
    pairs = SW_GROUP // 2
    dist = np.arange(W)[:, None] + W - np.arange(2 * W)[None, :]
    in_window = (dist >= 0) & (dist < W)
    onehot = (_t5_bucket(dist).reshape(-1)[None, :] == np.arange(REL_BUCKETS)[:, None]).astype(np.float32)
    bias = jnp.dot(rel_bias.astype(F32).T, onehot, precision=jax.lax.Precision.HIGHEST)
    bias = bias.reshape(SW_KV_HEADS, pairs, 2, W, 2 * W)
    bias = jnp.transpose(bias, (0, 2, 1, 3, 4)).reshape(SW_KV_HEADS, 2, pairs * W, 2 * W)
    mask_later = np.tile(in_window, (pairs, 1))
    mask_first = np.tile(in_window & (np.arange(2 * W) >= W)[None, :], (pairs, 1))
    return jnp.stack([jnp.where(mask_first, bias, -jnp.inf), jnp.where(mask_later, bias, -jnp.inf)])


def kernel(x, p, ffn1_norm, ffn1_wi, ffn1_wo, mix_norm, w_in, b_igate, b_fgate, ml_out_norm, q_norm, k_norm, sinks, rel_bias, w_a, w_b, w_out, ffn2_norm, ffn2_wi, ffn2_wo, ple_norm, w_ple_gate, w_ple):
    B, S, _ = x.shape
    depth = p.shape[0]
    n_tok = B * S
    tm = min(TOKEN_TILE, S)
    offs = [0] + [int(o) for o in np.cumsum(SPLITS)]

    bf = lambda w: w.astype(BF16)
    rows = lambda v: v.reshape(depth, 1, -1).astype(F32)

    def regroup(w_in_b):
        col = lambda a, b: w_in_b[:, :, offs[a]:offs[b]]
        wm = jnp.concatenate([col(0, 1), col(2, 4)], axis=2)
        wt = jnp.swapaxes(jnp.concatenate([col(1, 2), col(4, 6)], axis=2), 1, 2)
        wt = jnp.pad(wt, ((0, 0), (0, -wt.shape[1] % V7X_BF16_SUBLANES), (0, 0)))
        return wm, wt, col(6, 9), col(9, 11)

    qk_gain = jnp.concatenate([jnp.tile(q_norm, (1, SW_Q_HEADS)) * (SW_HEAD_DIM ** -0.5),
                               jnp.tile(k_norm, (1, SW_KV_HEADS))], axis=1).reshape(depth, 1, -1).astype(F32)
    gate_bias = jnp.concatenate([b_igate, b_fgate], axis=1).reshape(depth, -1, 1).astype(F32)
    sink = jnp.broadcast_to(sinks.astype(F32)[:, :, None], (depth, SW_Q_HEADS, V7X_LANES))
    ml_gain = ml_out_norm.astype(F32)
    f1g, f2g = rows(ffn1_norm), rows(ffn2_norm)
    f1wi, f1wo = bf(ffn1_wi[:1]), bf(ffn1_wo[:1])
    wm, wt, ws, wg = regroup(bf(w_in))
    mixg, pleg = rows(mix_norm), rows(ple_norm)
    wa, wb, wo, wpg, wpp = bf(w_a), bf(w_b), bf(w_out), bf(w_ple_gate), bf(w_ple)
    p3 = p.reshape(depth, n_tok, P_DIM)

    bias2 = _swa_bias(rel_bias)
    head_of = np.arange(V7X_MXU_DIM) // SW_HEAD_DIM
    group_ones = jnp.asarray(head_of[:, None] == head_of[None, :], BF16)

    flat = lambda a: a.reshape(n_tok, a.shape[-1])
    x2 = x.reshape(n_tok, D_MODEL)
    for i in range(depth):
        ahead = ((ffn2_wi, i), (ffn2_wo, i)) + (((ffn1_wi, i + 1), (ffn1_wo, i + 1)) if i + 1 < depth else ())
        x2, cast = _ffn(x2, f1g, f1wi, f1wo, i, 0, min(FFN_TOKEN_TILE, n_tok), cast_next=ahead)
        f2wi, f2wo = cast[:2]
        tok_major, gate, transposed, gate_rows = _inproj(
            x2.reshape(B, S, D_MODEL), mixg, wm, wt, gate_bias, ws, wg, qk_gain, group_ones, i,
            min(INPROJ_TOKEN_TILE, S))
        x2 = _mixpost(x2, flat(gate), p3, tok_major, transposed, gate_rows, bias2, sink, ml_gain,
                      wa, wb, wo, f2g, f2wi, f2wo, pleg, wpg, wpp, i, 0, S, tm)
        if len(cast) > 2:
            f1wi, f1wo = cast[2:]
    return x2.reshape(B, S, D_MODEL)
```

```python
import functools
import math

import jax
import jax.numpy as jnp
import numpy as np
from jax.experimental import pallas as pl
from jax.experimental.pallas import tpu as pltpu

D_MODEL = 1024
P_DIM = 256
D_FF = 2816
ML_HEADS = 4
ML_QK_DIM = 64
ML_V_DIM = 128
SW_Q_HEADS = 8
SW_KV_HEADS = 2
SW_HEAD_DIM = 64
WINDOW = 128
REL_BUCKETS = 32
REL_MAX_DIST = 128
EPS = 1e-6

ML_QK_W = ML_HEADS * ML_QK_DIM
ML_V_W = ML_HEADS * ML_V_DIM
SW_Q_W = SW_Q_HEADS * SW_HEAD_DIM
SW_KV_W = SW_KV_HEADS * SW_HEAD_DIM
SW_GROUP = SW_Q_HEADS // SW_KV_HEADS
SPLITS = (ML_QK_W, ML_QK_W, ML_V_W, ML_V_W, ML_HEADS, ML_HEADS, SW_Q_W, SW_KV_W, SW_KV_W, D_MODEL, D_MODEL)

TOK_MV = ML_QK_W
TOK_MO = TOK_MV + ML_V_W
TOK_SQ = TOK_MO + ML_V_W
TOK_SV = TOK_SQ + SW_Q_W
TOK_W = TOK_SV + SW_KV_W
TR_W = ML_QK_W + SW_KV_W

V7X_LANES = 128
V7X_MXU_DIM = 256
V7X_BF16_SUBLANES = 16
V7X_VMEM_BYTES = 64 * 1024 * 1024
V7X_VMEM_LIMIT_BYTES = V7X_VMEM_BYTES - 4 * 1024 * 1024

FFN_CHUNK = 256
ML_CHUNK = 128
TOKEN_TILE = 512
FFN_TOKEN_TILE = 1024
INPROJ_TOKEN_TILE = 1024

F32 = jnp.float32
BF16 = jnp.bfloat16


def _dot(a, b):
    return jnp.dot(a, b, preferred_element_type=F32)


def lax_rsqrt(v):
    return jax.lax.rsqrt(v)


def _rms_scale(xf):
    return lax_rsqrt(jnp.mean(xf * xf, axis=-1, keepdims=True) + EPS)


def _sigmoid(v):
    return 0.5 * (1.0 + jnp.tanh(0.5 * v))


def _const_spec(shape):
    nd = len(shape)
    return pl.BlockSpec(shape, lambda *_: (0,) * nd, pipeline_mode=pl.Buffered(1))


def _layer_spec(stacked, layer):
    tail = stacked.shape[1:]
    return pl.BlockSpec((None,) + tail, lambda *_: (layer,) + (0,) * len(tail), pipeline_mode=pl.Buffered(1))


def _params(*sem):
    return pltpu.CompilerParams(dimension_semantics=sem, vmem_limit_bytes=V7X_VMEM_LIMIT_BYTES)


def _ffn_body(x_ref, g_ref, wi_ref, wo_ref, *rest, n_cast):
    src, o_ref, dst = rest[:n_cast], rest[n_cast], rest[n_cast + 1:2 * n_cast + 1]
    n_ref, act_ref = rest[2 * n_cast + 1:]
    xf = x_ref[...]
    n_ref[...] = (xf * _rms_scale(xf) * g_ref[...]).astype(BF16)
    for j in range(D_FF // FFN_CHUNK):
        lo = j * FFN_CHUNK
        hg = _dot(n_ref[...], wi_ref[:, lo:lo + FFN_CHUNK])
        hu = _dot(n_ref[...], wi_ref[:, D_FF + lo:D_FF + lo + FFN_CHUNK])
        act_ref[:, lo:lo + FFN_CHUNK] = (hg * _sigmoid(hg) * hu).astype(BF16)
        if j % 2 == 1 and j // 2 < n_cast:
            dst[j // 2][...] = src[j // 2][...].astype(BF16)
    o_ref[...] = xf + 0.5 * _dot(act_ref[...], wo_ref[...])


def _ffn(x2, g, wi, wo, layer, w_layer, tm, cast_next=()):
    n_tok = x2.shape[0]
    steps = n_tok // tm
    cast_in, cast_out, cast_shape = [], [], []
    for w, slab in cast_next:
        rows = w.shape[1]
        blocks = max(k for k in range(1, steps + 1) if steps % k == 0 and rows % (k * V7X_BF16_SUBLANES) == 0)
        per = steps // blocks
        block = (None, rows // blocks, w.shape[2])
        cast_in.append(pl.BlockSpec(block, lambda i, per=per, slab=slab: (slab, i // per, 0)))
        cast_out.append(pl.BlockSpec(block, lambda i, per=per: (0, i // per, 0)))
        cast_shape.append(jax.ShapeDtypeStruct((1,) + w.shape[1:], BF16))
    tok = pl.BlockSpec((tm, D_MODEL), lambda i: (i, 0))
    outs = pl.pallas_call(
        functools.partial(_ffn_body, n_cast=len(cast_next)),
        grid=(steps,),
        in_specs=[tok, _layer_spec(g, layer), _layer_spec(wi, w_layer), _layer_spec(wo, w_layer)] + cast_in,
        out_specs=[tok] + cast_out,
        out_shape=[jax.ShapeDtypeStruct((n_tok, D_MODEL), F32)] + cast_shape,
        scratch_shapes=[pltpu.VMEM((tm, D_MODEL), BF16), pltpu.VMEM((tm, D_FF), BF16)],
        compiler_params=_params("arbitrary" if cast_next else "parallel"),
        name="ffn",
    )(x2, g, wi, wo, *[w for w, _ in cast_next])
    return outs[0], tuple(outs[1:])


def _inproj_body(x_ref, g_ref, wm_ref, wt_ref, bif_ref, ws_ref, wg_ref, gain_ref, ones_ref,
                 tok_ref, gate_ref, tr_ref, gt_ref):
    xf = x_ref[...]
    u = (xf * _rms_scale(xf) * g_ref[...]).astype(BF16)
    tok_ref[:, :TOK_SQ] = _dot(u, wm_ref[...]).astype(BF16)
    nt = (((1,), (1,)), ((), ()))
    zt = jax.lax.dot_general(wt_ref[...], u, nt, preferred_element_type=F32)
    tr_ref[:ML_QK_W, :] = (zt[:ML_QK_W] * (1.0 / math.sqrt(ML_QK_DIM))).astype(BF16)
    pre = zt[ML_QK_W:ML_QK_W + 2 * ML_HEADS] + bif_ref[...]
    logf = jnp.minimum(pre, 0.0) - jnp.log1p(jnp.exp(-jnp.abs(pre)))
    pos = jax.lax.broadcasted_iota(jnp.int32, pre.shape, 1) & (ML_CHUNK - 1)
    csum = logf
    k = 1
    while k < ML_CHUNK:
        csum = csum + jnp.where(pos >= k, pltpu.roll(csum, k, 1), 0.0)
        k *= 2
    gt_ref[:2 * ML_HEADS, :] = jnp.where(jax.lax.broadcasted_iota(jnp.int32, pre.shape, 0) < ML_HEADS, pre, logf)
    gt_ref[2 * ML_HEADS:, :] = csum
    zs = _dot(u, ws_ref[...])
    qk = zs[:, :SW_Q_W + SW_KV_W]
    sq2 = (qk * qk).astype(BF16)
    slab = ones_ref.shape[0]
    widths = [min(slab, sq2.shape[1] - lo) for lo in range(0, sq2.shape[1], slab)]
    ssq = jnp.concatenate([_dot(sq2[:, i * slab:i * slab + w], ones_ref[:w, :w]) for i, w in enumerate(widths)],
                          axis=1)
    qkn = qk * lax_rsqrt(ssq * (1.0 / SW_HEAD_DIM) + EPS) * gain_ref[...]
    tok_ref[:, TOK_SQ:TOK_SV] = qkn[:, :SW_Q_W].astype(BF16)
    tr_ref[ML_QK_W:, :] = qkn[:, SW_Q_W:].T.astype(BF16)
    tok_ref[:, TOK_SV:] = zs[:, SW_Q_W + SW_KV_W:].astype(BF16)
    gate_ref[...] = _dot(u, wg_ref[...]).astype(BF16)


def _inproj(x3, g, wm, wt, bif, ws, wg, gain, ones, layer, tm):
    b, s, _ = x3.shape
    assert tm % ML_CHUNK == 0
    tok = lambda w: pl.BlockSpec((None, tm, w), lambda i, j: (i, j, 0))
    tok_t = lambda w: pl.BlockSpec((None, w, tm), lambda i, j: (i, 0, j))
    sd = jax.ShapeDtypeStruct
    return pl.pallas_call(
        _inproj_body,
        grid=(b, s // tm),
        in_specs=[tok(D_MODEL)] + [_layer_spec(a, layer) for a in (g, wm, wt, bif, ws, wg, gain)]
                 + [_const_spec(ones.shape)],
        out_specs=[tok(TOK_W), tok(2 * D_MODEL), tok_t(TR_W), tok_t(4 * ML_HEADS)],
        out_shape=[sd((b, s, TOK_W), BF16), sd((b, s, 2 * D_MODEL), BF16), sd((b, TR_W, s), BF16),
                   sd((b, 4 * ML_HEADS, s), F32)],
        compiler_params=_params("parallel", "parallel"),
        name="inproj",
    )(x3, g, wm, wt, bif, ws, wg, gain, ones)


def _mlstm_pieces(q_ref, kt_ref, v_ref, o_ref, gl_ref, gc_ref, gain_ref, h_ref, c_ref, m_ref):
    L = ML_CHUNK
    H, dk, dv = ML_HEADS, ML_QK_DIM, ML_V_DIM
    heads = range(H)
    causal = (jax.lax.broadcasted_iota(jnp.int32, (L, L), 1)
              <= jax.lax.broadcasted_iota(jnp.int32, (L, L), 0))
    ones_col = (jax.lax.broadcasted_iota(jnp.int32, (L, V7X_LANES), 1) == 0).astype(BF16)
    q_head = jax.lax.broadcasted_iota(jnp.int32, (L, H * dk), 1) // dk
    zeros_c = jnp.zeros((dk, 2 * dv), BF16)
    zeros_k = jnp.zeros((dk, 2 * dk), BF16)

    def chunk(c):
        sl = slice(c * L, (c + 1) * L)
        st = {}

        def gates():
            gl = gl_ref[:, sl]
            gc = gc_ref[:, sl]
            b_rows = [gc[H + h:H + h + 1, :] for h in heads]
            c_rows = [gl[h:h + 1, :] - b_rows[h] for h in heads]
            st["m_prev"] = [m_ref[h:h + 1, 0:1] for h in heads]
            st["g_tot"] = [b_rows[h][:, L - 1:L] for h in heads]
            st["a_row"] = [st["g_tot"][h] + c_rows[h] for h in heads]
            st["m_new"] = [jnp.maximum(st["g_tot"][h] + st["m_prev"][h],
                                       jnp.max(st["a_row"][h], axis=-1, keepdims=True)) for h in heads]
            st["c_rows"] = c_rows
            st["f_rows"] = [gl[H + h:H + h + 1, :] for h in heads]
            for h in heads:
                m_ref[h:h + 1, :] = jnp.broadcast_to(st["m_new"][h], (1, V7X_LANES))

        def scores():
            q = q_ref[sl, :]
            q_sel = jnp.concatenate([jnp.where(q_head == h, q, jnp.zeros((), BF16)) for h in heads], axis=0)
            st["s_all"] = _dot(q_sel, kt_ref[:, sl])

        st["m_col"], st["b_col"], st["w"], st["res"] = {}, {}, {}, {}

        def weights(h):
            def run():
                dm = jnp.where(causal, st["c_rows"][h], -jnp.inf)
                m_col = jnp.maximum(jnp.max(dm, axis=-1, keepdims=True), st["m_prev"][h])
                st["m_col"][h] = m_col
                st["b_col"][h] = jnp.sum(jnp.where(causal, st["f_rows"][h], 0.0), axis=-1, keepdims=True)
                st["w"][h] = (jnp.exp(dm - m_col) * st["s_all"][h * L:(h + 1) * L, :]).astype(BF16)
            return run

        def head_matmul(h):
            def run():
                pair = h // 2
                v_aug = jnp.concatenate([v_ref[sl, h * dv:(h + 1) * dv], ones_col], axis=1)
                st.setdefault("c_aug", {})[h] = c_ref[h]
                c_bf = st["c_aug"][h].astype(BF16)
                rhs = jnp.concatenate([v_aug] + ([c_bf, zeros_c] if h % 2 == 0 else [zeros_c, c_bf]), axis=0)
                s_inter = jnp.exp(st["m_prev"][h] - st["m_col"][h])
                q_s = (q_ref[sl, pair * 2 * dk:(pair + 1) * 2 * dk].astype(F32) * s_inter).astype(BF16)
                kw = (kt_ref[h * dk:(h + 1) * dk, sl].astype(F32)
                      * jnp.exp(st["a_row"][h] - st["m_new"][h])).astype(BF16)
                lhs = jnp.concatenate([jnp.concatenate([st["w"][h], q_s], axis=1),
                                       jnp.concatenate([kw, zeros_k], axis=1)], axis=0)
                st["res"][h] = _dot(lhs, rhs)
            return run

        def update(h):
            def run():
                decay = jnp.exp(st["g_tot"][h] + st["m_prev"][h] - st["m_new"][h])
                c_ref[h] = decay * st["c_aug"][h] + st["res"][h][L:, :]
            return run

        def output(h):
            def run():
                res = st["res"][h]
                inv = 1.0 / jnp.maximum(jnp.abs(res[:L, dv:dv + 1]), jnp.exp(-(st["b_col"][h] + st["m_col"][h])))
                hh = res[:L, :dv] * inv
                hn = hh * lax_rsqrt(jnp.mean(hh * hh, axis=-1, keepdims=True) + EPS) * gain_ref[h:h + 1, :]
                gate = _sigmoid(o_ref[sl, h * dv:(h + 1) * dv].astype(F32))
                h_ref[sl, h * dv:(h + 1) * dv] = (gate * hn).astype(BF16)
            return run

        return ([gates, scores] + [f(h) for h in heads for f in (weights, head_matmul)]
                + [f(h) for h in heads for f in (update, output)])

    pieces = []
    for c in range(q_ref.shape[0] // L):
        pieces += chunk(c)
    return pieces


def _swa_pieces(q_ref, kt_ref, ktp_ref, v_ref, vp_ref, bias_ref, first_sel, sink_ref, o_ref):
    W, d = WINDOW, SW_HEAD_DIM
    pairs = SW_GROUP // 2
    rows = pairs * W
    groups = range(SW_KV_HEADS)
    lane = jax.lax.broadcasted_iota(jnp.int32, (rows, 2 * d), 1)
    zero = jnp.zeros((), BF16)
    ones_v = jnp.ones((2 * W, 2 * d), BF16)

    def block(blk):
        cur = slice(blk * W, (blk + 1) * W)
        prev = slice((blk - 1) * W, blk * W)
        st = {}

        st.update(scores={}, vv={}, probs={}, tail={})

        def scores(g):
            def run():
                hd = slice(g * d, (g + 1) * d)
                kt_prev = ktp_ref[hd, :] if blk == 0 else kt_ref[hd, prev]
                v_prev = vp_ref[:, hd] if blk == 0 else v_ref[prev, hd]
                kt = jnp.concatenate([kt_prev, kt_ref[hd, cur]], axis=1)
                vg = jnp.concatenate([v_prev, v_ref[cur, hd]], axis=0)
                st["vv"][g] = jnp.concatenate([vg, vg, ones_v], axis=1)
                q_rows = jnp.concatenate([q_ref[cur, (g * pairs + pr) * 2 * d:(g * pairs + pr + 1) * 2 * d]
                                          for pr in range(pairs)], axis=0)
                q_sel = jnp.concatenate([jnp.where(lane < d, q_rows, zero), jnp.where(lane < d, zero, q_rows)],
                                        axis=0)
                st["scores"][g] = _dot(q_sel, jnp.concatenate([kt, kt], axis=0))
            return run

        def softmax(g, sub):
            def run():
                bias = bias_ref[first_sel, g, sub] if blk == 0 else bias_ref[1, g, sub]
                logits = st["scores"][g][sub * rows:(sub + 1) * rows, :] + bias
                sink = jnp.concatenate(
                    [jnp.broadcast_to(sink_ref[g * SW_GROUP + 2 * pr + sub:g * SW_GROUP + 2 * pr + sub + 1, :],
                                      (W, V7X_LANES)) for pr in range(pairs)], axis=0)
                m = jnp.maximum(jnp.broadcast_to(jnp.max(logits, axis=-1, keepdims=True), (rows, V7X_LANES)),
                                sink)
                st["probs"][g, sub] = jnp.exp(logits - jnp.concatenate([m, m], axis=1)).astype(BF16)
                st["tail"][g, sub] = jnp.exp(sink - m)
            return run

        def values(g):
            def run():
                probs = jnp.concatenate([st["probs"][g, 0], st["probs"][g, 1]], axis=0)
                res = _dot(probs, st["vv"][g])
                outs = [res[sub * rows:(sub + 1) * rows, :2 * d]
                        / (res[sub * rows:(sub + 1) * rows, 2 * d:] + st["tail"][g, sub]) for sub in range(2)]
                sel = jnp.where(lane < d, outs[0], outs[1]).astype(BF16)
                for pr in range(pairs):
                    o_ref[cur, (g * pairs + pr) * 2 * d:(g * pairs + pr + 1) * 2 * d] = sel[pr * W:(pr + 1) * W]
            return run

        return ([scores(g) for g in groups] + [softmax(g, sub) for g in groups for sub in range(2)]
                + [values(g) for g in groups])

    pieces = []
    for blk in range(q_ref.shape[0] // W):
        pieces += block(blk)
    return pieces


def _merge_lists(a, b):
    out, ia, ib = [], 0, 0
    while ia < len(a) or ib < len(b):
        if ib >= len(b) or (ia < len(a) and ia * len(b) <= ib * len(a)):
            out.append(a[ia]); ia += 1
        else:
            out.append(b[ib]); ib += 1
    return out


def _mixpost_body(x_ref, gate_ref, p_ref, tok_ref, tr_ref, gt_ref,
                  bias_ref, sink_ref, mlg_ref, wa_ref, wb_ref, wo_ref, fg_ref, fwi_ref, fwo_ref,
                  pg_ref, pwg_ref, pwp_ref, o_ref,
                  n_ref, act_ref, x1_ref, ha_ref, hb_ref, c_ref, m_ref, ktp_ref, vp_ref, *, tiles_per_seq):
    t = pl.program_id(0)
    n_tiles = pl.num_programs(0) - 1
    seq_start = jnp.minimum(t, n_tiles - 1) % tiles_per_seq == 0

    @pl.when(seq_start)
    def _():
        c_ref[...] = jnp.zeros_like(c_ref)
        m_ref[...] = jnp.zeros_like(m_ref)
        ktp_ref[...] = jnp.zeros_like(ktp_ref)
        vp_ref[...] = jnp.zeros_like(vp_ref)

    first_sel = jnp.where(seq_start, 0, 1)
    cols = lambda lo, hi: tok_ref.at[:, lo:hi]
    mq_ref, mv_ref, mo_ref = cols(0, TOK_MV), cols(TOK_MV, TOK_MO), cols(TOK_MO, TOK_SQ)
    sq_ref, sv_ref = cols(TOK_SQ, TOK_SV), cols(TOK_SV, TOK_W)
    mkt_ref, skt_ref = tr_ref.at[:ML_QK_W, :], tr_ref.at[ML_QK_W:, :]
    gl_ref, gc_ref = gt_ref.at[:2 * ML_HEADS, :], gt_ref.at[2 * ML_HEADS:, :]

    def keep_last_block():
        ktp_ref[...] = skt_ref[:, skt_ref.shape[1] - WINDOW:]
        vp_ref[...] = sv_ref[sv_ref.shape[0] - WINDOW:, :]

    def mixer_pieces():
        return _merge_lists(
            _mlstm_pieces(mq_ref, mkt_ref, mv_ref, mo_ref, gl_ref, gc_ref, mlg_ref, ha_ref, c_ref, m_ref),
            _swa_pieces(sq_ref, skt_ref, ktp_ref, sv_ref, vp_ref, bias_ref, first_sel, sink_ref, hb_ref)
        ) + [keep_last_block]

    def merge_gated():
        ya = _dot(ha_ref[...], wa_ref[...])
        yb = _dot(hb_ref[...], wb_ref[...])
        return (_sigmoid(gate_ref[:, :D_MODEL].astype(F32)) * ya
                + _sigmoid(gate_ref[:, D_MODEL:].astype(F32)) * yb).astype(BF16)

    def carry(x1):
        x1_ref[...] = x1
        n_ref[...] = (x1 * _rms_scale(x1) * fg_ref[...]).astype(BF16)

    @pl.when(t == 0)
    def _():
        for piece in mixer_pieces():
            piece()
        carry(x_ref[...] + _dot(merge_gated(), wo_ref[...]))

    @pl.when(t > 0)
    def _():
        pieces = mixer_pieces()
        n_chunks = D_FF // FFN_CHUNK
        n_slots = 2 * n_chunks

        def run_slot(k):
            for piece in pieces[k * len(pieces) // n_slots:(k + 1) * len(pieces) // n_slots]:
                piece()

        for j in range(n_chunks):
            lo = j * FFN_CHUNK
            hg = _dot(n_ref[...], fwi_ref[:, lo:lo + FFN_CHUNK])
            run_slot(2 * j)
            hu = _dot(n_ref[...], fwi_ref[:, D_FF + lo:D_FF + lo + FFN_CHUNK])
            act_ref[:, lo:lo + FFN_CHUNK] = (hg * _sigmoid(hg) * hu).astype(BF16)
            run_slot(2 * j + 1)
        x2 = x1_ref[...] + 0.5 * _dot(act_ref[...], fwo_ref[...])
        mixed = merge_gated()
        n_ple = (x2 * _rms_scale(x2) * pg_ref[...]).astype(BF16)
        x1 = x_ref[...] + _dot(mixed, wo_ref[...])
        gate = _sigmoid(_dot(n_ple, pwg_ref[...]))
        o_ref[...] = x2 + gate * _dot(p_ref[...].astype(BF16), pwp_ref[...])
        carry(x1)


def _mixpost(x2, gate, p3, tok_major, transposed, gate_rows, bias2, sink, ml_gain,
             wa, wb, wo, fg, fwi, fwo, pg, pwg, pwp, layer, ffn_w_layer, seq_len, tm):
    n_tok = x2.shape[0]
    n_tiles = n_tok // tm
    tps = seq_len // tm
    assert tm % WINDOW == 0 and tm % ML_CHUNK == 0 and seq_len % tm == 0
    mix = lambda t: jnp.minimum(t, n_tiles - 1)
    post = lambda t: jnp.maximum(t - 1, 0)
    tok = lambda w: pl.BlockSpec((tm, w), lambda t: (post(t), 0))
    tok_mix = lambda w: pl.BlockSpec((tm, w), lambda t: (mix(t), 0))
    seq = lambda w: pl.BlockSpec((None, tm, w), lambda t: (mix(t) // tps, mix(t) % tps, 0))
    seq_t = lambda w: pl.BlockSpec((None, w, tm), lambda t: (mix(t) // tps, 0, mix(t) % tps))
    weights = (wa, wb, wo, fg, fwi, fwo, pg, pwg, pwp)
    return pl.pallas_call(
        functools.partial(_mixpost_body, tiles_per_seq=tps),
        grid=(n_tiles + 1,),
        in_specs=[tok_mix(D_MODEL), tok_mix(2 * D_MODEL),
                  pl.BlockSpec((None, tm, P_DIM), lambda t: (layer, post(t), 0)),
                  seq(TOK_W), seq_t(TR_W), seq_t(4 * ML_HEADS),
                  _const_spec(bias2.shape), _layer_spec(sink, layer), _layer_spec(ml_gain, layer)]
                 + [_layer_spec(w, ffn_w_layer if w is fwi or w is fwo else layer) for w in weights],
        out_specs=tok(D_MODEL),
        out_shape=jax.ShapeDtypeStruct((n_tok, D_MODEL), F32),
        scratch_shapes=[pltpu.VMEM((tm, D_MODEL), BF16), pltpu.VMEM((tm, D_FF), BF16),
                        pltpu.VMEM((tm, D_MODEL), F32),
                        pltpu.VMEM((tm, ML_V_W), BF16), pltpu.VMEM((tm, SW_Q_W), BF16),
                        pltpu.VMEM((ML_HEADS, ML_QK_DIM, 2 * ML_V_DIM), F32),
                        pltpu.VMEM((2 * ML_HEADS, V7X_LANES), F32),
                        pltpu.VMEM((SW_KV_W, WINDOW), BF16), pltpu.VMEM((WINDOW, SW_KV_W), BF16)],
        compiler_params=_params("arbitrary"),
        name="mixpost",
    )(x2, gate, p3, tok_major, transposed, gate_rows, bias2, sink, ml_gain, *weights)


def _t5_bucket(dist):
    max_exact = REL_BUCKETS // 2
    d = np.maximum(dist, 0)
    large = max_exact + (np.log(np.maximum(d, 1) / max_exact) / np.log(REL_MAX_DIST / max_exact)
                         * (REL_BUCKETS - max_exact)).astype(np.int32)
    large = np.minimum(large, REL_BUCKETS - 1)
    return np.where(d < max_exact, d, large).astype(np.int32)


def _swa_bias(rel_bias):
    W = WINDOW
    pairs = SW_GROUP // 2
    dist = np.arange(W)[:, None] + W - np.arange(2 * W)[None, :]
    in_window = (dist >= 0) & (dist < W)
    onehot = (_t5_bucket(dist).reshape(-1)[None, :] == np.arange(REL_BUCKETS)[:, None]).astype(np.float32)
    bias = jnp.dot(rel_bias.astype(F32).T, onehot, precision=jax.lax.Precision.HIGHEST)
    bias = bias.reshape(SW_KV_HEADS, pairs, 2, W, 2 * W)
    bias = jnp.transpose(bias, (0, 2, 1, 3, 4)).reshape(SW_KV_HEADS, 2, pairs * W, 2 * W)
    mask_later = np.tile(in_window, (pairs, 1))
    mask_first = np.tile(in_window & (np.arange(2 * W) >= W)[None, :], (pairs, 1))
    return jnp.stack([jnp.where(mask_first, bias, -jnp.inf), jnp.where(mask_later, bias, -jnp.inf)])


def kernel(x, p, ffn1_norm, ffn1_wi, ffn1_wo, mix_norm, w_in, b_igate, b_fgate, ml_out_norm, q_norm, k_norm, sinks, rel_bias, w_a, w_b, w_out, ffn2_norm, ffn2_wi, ffn2_wo, ple_norm, w_ple_gate, w_ple):
    B, S, _ = x.shape
    depth = p.shape[0]
    n_tok = B * S
    tm = min(TOKEN_TILE, S)
    offs = [0] + [int(o) for o in np.cumsum(SPLITS)]

    bf = lambda w: w.astype(BF16)
    rows = lambda v: v.reshape(depth, 1, -1).astype(F32)

    def regroup(w_in_b):
        col = lambda a, b: w_in_b[:, :, offs[a]:offs[b]]
        wm = jnp.concatenate([col(0, 1), col(2, 4)], axis=2)
        wt = jnp.swapaxes(jnp.concatenate([col(1, 2), col(4, 6)], axis=2), 1, 2)
        wt = jnp.pad(wt, ((0, 0), (0, -wt.shape[1] % V7X_BF16_SUBLANES), (0, 0)))
        return wm, wt, col(6, 9), col(9, 11)

    qk_gain = jnp.concatenate([jnp.tile(q_norm, (1, SW_Q_HEADS)) * (SW_HEAD_DIM ** -0.5),
                               jnp.tile(k_norm, (1, SW_KV_HEADS))], axis=1).reshape(depth, 1, -1).astype(F32)
    gate_bias = jnp.concatenate([b_igate, b_fgate], axis=1).reshape(depth, -1, 1).astype(F32)
    sink = jnp.broadcast_to(sinks.astype(F32)[:, :, None], (depth, SW_Q_HEADS, V7X_LANES))
    ml_gain = ml_out_norm.astype(F32)
    f1g, f2g = rows(ffn1_norm), rows(ffn2_norm)
    f1wi, f1wo = bf(ffn1_wi[:1]), bf(ffn1_wo[:1])
    wm, wt, ws, wg = regroup(bf(w_in))
    mixg, pleg = rows(mix_norm), rows(ple_norm)
    wa, wb, wo, wpg, wpp = bf(w_a), bf(w_b), bf(w_out), bf(w_ple_gate), bf(w_ple)
    p3 = p.reshape(depth, n_tok, P_DIM)

    bias2 = _swa_bias(rel_bias)
    head_of = np.arange(V7X_MXU_DIM) // SW_HEAD_DIM
    group_ones = jnp.asarray(head_of[:, None] == head_of[None, :], BF16)

    flat = lambda a: a.reshape(n_tok, a.shape[-1])
    x2 = x.reshape(n_tok, D_MODEL)
    for i in range(depth):
        ahead = ((ffn2_wi, i), (ffn2_wo, i)) + (((ffn1_wi, i + 1), (ffn1_wo, i + 1)) if i + 1 < depth else ())
        x2, cast = _ffn(x2, f1g, f1wi, f1wo, i, 0, min(FFN_TOKEN_TILE, n_tok), cast_next=ahead)
        f2wi, f2wo = cast[:2]
        tok_major, gate, transposed, gate_rows = _inproj(
            x2.reshape(B, S, D_MODEL), mixg, wm, wt, gate_bias, ws, wg, qk_gain, group_ones, i,
            min(INPROJ_TOKEN_TILE, S))
        x2 = _mixpost(x2, flat(gate), p3, tok_major, transposed, gate_rows, bias2, sink, ml_gain,
                      wa, wb, wo, f2g, f2wi, f2wo, pleg, wpg, wpp, i, 0, S, tm)
        if len(cast) > 2:
            f1wi, f1wo = cast[2:]
    return x2.reshape(B, S, D_MODEL)
```
